```python
import math, functools
import jax, jax.numpy as jnp
from jax import lax
import numpy as np

D_MODEL = 4096
BATCH = 4
SEQ = 2048
DEPTH = 2
DEC_BATCH = 8
DEC_SEQ = 4
PAST_LEN = 16384
PAGE_SIZE = 128

N_BRANCH = 4
BRANCH_WIDTH = D_MODEL // N_BRANCH
HEAD_DIM = 128
SC_WIDTH = BRANCH_WIDTH
SC_CONV_W = 3
DIFF_HEADS = BRANCH_WIDTH // (2 * HEAD_DIM)
DIFF_VDIM = 2 * HEAD_DIM
DSA_HEADS = BRANCH_WIDTH // HEAD_DIM
IDX_HEADS = 8
IDX_DIM = 64
TOPK_MAX = 256
CONF_WIDTH = BRANCH_WIDTH
CONF_CONV_W = 31
FFN_HIDDEN = -(-8 * D_MODEL // (3 * 256)) * 256
ROPE_THETA = 10000.0
NORM_EPS = 1e-6
ATTN_QBLOCK = 128
DSA_QBLOCK = 64
IN_SPLIT_SIZES = (SC_WIDTH, SC_WIDTH, SC_WIDTH,
                  2 * DIFF_HEADS * HEAD_DIM, 2 * DIFF_HEADS * HEAD_DIM, DIFF_HEADS * DIFF_VDIM,
                  DSA_HEADS * HEAD_DIM, DSA_HEADS * HEAD_DIM, DSA_HEADS * HEAD_DIM,
                  IDX_HEADS * IDX_DIM, IDX_DIM, IDX_HEADS,
                  2 * CONF_WIDTH)

kernel_name = 'hybrid_gated_branch_decoder_step'


def rms_norm(x, g):
    x32 = x.astype(jnp.float32)
    y = x32 * lax.rsqrt(jnp.mean(x32 * x32, axis=-1, keepdims=True) + NORM_EPS)
    return (y * g.astype(jnp.float32)).astype(x.dtype)


def layer_norm(x, g, b):
    x32 = x.astype(jnp.float32)
    xc = x32 - jnp.mean(x32, axis=-1, keepdims=True)
    y = xc * lax.rsqrt(jnp.mean(xc * xc, axis=-1, keepdims=True) + NORM_EPS)
    return (y * g.astype(jnp.float32) + b.astype(jnp.float32)).astype(x.dtype)


def rope(x, pos):
    d = x.shape[-1]
    inv = ROPE_THETA ** (-jnp.arange(0, d, 2, dtype=jnp.float32) / d)
    ang = pos.astype(jnp.float32)[:, None] * inv[None, :]
    shape = (1, pos.shape[0]) + (1,) * (x.ndim - 3) + (d // 2,)
    cos = jnp.cos(ang).reshape(shape).astype(x.dtype)
    sin = jnp.sin(ang).reshape(shape).astype(x.dtype)
    x1, x2 = x[..., : d // 2], x[..., d // 2:]
    return jnp.concatenate([x1 * cos - x2 * sin, x2 * cos + x1 * sin], axis=-1)


def causal_dwconv(x, state, w):
    width = w.shape[0]
    xp = jnp.concatenate([state.astype(x.dtype), x], axis=1)
    y = lax.conv_general_dilated(xp, w[:, None, :].astype(x.dtype), window_strides=(1,), padding='VALID',
                                 dimension_numbers=('NWC', 'WIO', 'NWC'), feature_group_count=x.shape[-1])
    return y, xp[:, xp.shape[1] - (width - 1):]


def take_rows(a, idx):
    return jax.vmap(lambda ab, ib: ab[ib])(a, idx)


def sweep_query_blocks(fn, q_inputs, q_pos, block):
    t = q_pos.shape[0]
    block = min(block, t)
    nb = t // block

    def split(a):
        return jnp.moveaxis(a.reshape((a.shape[0], nb, block) + a.shape[2:]), 1, 0)

    out = lax.map(lambda args: fn(*args), tuple(split(a) for a in q_inputs) + (q_pos.reshape(nb, block),))
    out = jnp.moveaxis(out, 0, 1)
    return out.reshape((out.shape[0], t) + out.shape[3:])


def diff_attn_core(q, k, v, q_pos, k_pos, lam):
    s = jnp.einsum('bqhcd,bshcd->bhcqs', q, k).astype(jnp.float32) * (HEAD_DIM ** -0.5)
    mask = k_pos[None, :] <= q_pos[:, None]
    a = jax.nn.softmax(jnp.where(mask, s, -jnp.inf), axis=-1)
    p = a[:, :, 0] - lam * a[:, :, 1]
    return jnp.einsum('bhqs,bshe->bqhe', p.astype(v.dtype), v)


def dsa_core(q, qi, wi, q_pos, k_idx, k_pos, topk, gather_kv):
    sc = jax.nn.relu(jnp.einsum('bqhd,bsd->bqhs', qi, k_idx).astype(jnp.float32) * (IDX_DIM ** -0.5))
    score = jnp.einsum('bqh,bqhs->bqs', wi.astype(jnp.float32), sc)
    mask = k_pos[None, :] <= q_pos[:, None]
    score = jnp.where(mask[None], score, -jnp.inf)
    _, idx = lax.top_k(score, topk)
    valid = jnp.take(k_pos, idx) <= q_pos[None, :, None]
    k_sel, v_sel = gather_kv(idx)
    s = jnp.einsum('bqhd,bqkhd->bqhk', q, k_sel).astype(jnp.float32) * (HEAD_DIM ** -0.5)
    a = jax.nn.softmax(jnp.where(valid[:, :, None, :], s, -jnp.inf), axis=-1)
    return jnp.einsum('bqhk,bqkhd->bqhd', a.astype(v_sel.dtype), v_sel)


def paged_rows(cache, page_table):
    b, n_pages = page_table.shape
    return cache[page_table].reshape((b, n_pages * cache.shape[1]) + cache.shape[2:])


def prompt_diff_attend(q, k, v, lam, pos):
    fn = lambda qb, pb: diff_attn_core(qb, k, v, pb, pos, lam)
    return sweep_query_blocks(fn, (q,), pos, ATTN_QBLOCK)


def sample_diff_attend(q, k, v, lam, cache_k, cache_v, page_table, q_pos, k_pos):
    k_all = jnp.concatenate([paged_rows(cache_k, page_table).astype(k.dtype), k], axis=1)
    v_all = jnp.concatenate([paged_rows(cache_v, page_table).astype(v.dtype), v], axis=1)
    return diff_attn_core(q, k_all, v_all, q_pos, k_pos, lam)


def prompt_dsa_attend(q, k, v, qi, ki, wi, pos, topk):
    gather = lambda idx: (take_rows(k, idx), take_rows(v, idx))
    fn = lambda qb, qib, wib, pb: dsa_core(qb, qib, wib, pb, ki, pos, topk, gather)
    return sweep_query_blocks(fn, (q, qi, wi), pos, DSA_QBLOCK)


def sample_dsa_attend(q, k, v, qi, ki, wi, cache_k, cache_v, cache_kidx, page_table, q_pos, k_pos, topk):
    b, t = q.shape[:2]
    past = page_table.shape[1] * PAGE_SIZE
    ki_all = jnp.concatenate([paged_rows(cache_kidx, page_table).astype(ki.dtype), ki], axis=1)

    def gather(idx):
        in_past = (idx < past)[..., None, None]
        ip = jnp.minimum(idx, past - 1)
        phys = page_table[jnp.arange(b)[:, None, None], ip // PAGE_SIZE]
        off = ip % PAGE_SIZE
        inew = jnp.clip(idx - past, 0, t - 1)
        k_sel = jnp.where(in_past, cache_k[phys, off].astype(k.dtype), take_rows(k, inew))
        v_sel = jnp.where(in_past, cache_v[phys, off].astype(v.dtype), take_rows(v, inew))
        return k_sel, v_sel

    return dsa_core(q, qi, wi, q_pos, ki_all, k_pos, topk, gather)


def layer_forward(x, pos, sc_state, cf_state, attend_diff, attend_dsa, lam_init,
                  norm_mix, w_in, conv_sc, qk_norm, diff_lambda, diff_subln, conv_conf, conv_conf_bias,
                  conf_ln, w_gate, w_branch, w_out, norm_ffn, w_ffn_gate, w_ffn_up, w_ffn_down):
    b, t, _ = x.shape
    h = rms_norm(x, norm_mix)
    u = h @ w_in
    split_points = [int(s) for s in np.cumsum(IN_SPLIT_SIZES)[:-1]]
    (sc_in, sc_b, sc_c, dq, dk, dv, sq, sk, sv, iq, ik, iw, cf) = jnp.split(u, split_points, axis=-1)

    z, sc_new = causal_dwconv(sc_c * sc_in, sc_state, conv_sc)
    y_a = sc_b * z

    dq = rope(rms_norm(dq.reshape(b, t, DIFF_HEADS, 2, HEAD_DIM), qk_norm[0]), pos)
    dk = rope(rms_norm(dk.reshape(b, t, DIFF_HEADS, 2, HEAD_DIM), qk_norm[1]), pos)
    dv = dv.reshape(b, t, DIFF_HEADS, DIFF_VDIM)
    lv = diff_lambda.astype(jnp.float32)
    lam = jnp.exp(jnp.sum(lv[0] * lv[1])) - jnp.exp(jnp.sum(lv[2] * lv[3])) + lam_init
    o = attend_diff(dq, dk, dv, lam)
    y_b = (rms_norm(o, diff_subln) * (1.0 - lam_init)).reshape(b, t, DIFF_HEADS * DIFF_VDIM)

    sq = rope(rms_norm(sq.reshape(b, t, DSA_HEADS, HEAD_DIM), qk_norm[2]), pos)
    sk = rope(rms_norm(sk.reshape(b, t, DSA_HEADS, HEAD_DIM), qk_norm[3]), pos)
    sv = sv.reshape(b, t, DSA_HEADS, HEAD_DIM)
    iq = rope(iq.reshape(b, t, IDX_HEADS, IDX_DIM), pos)
    ik = rope(ik, pos)
    iw = iw * (IDX_HEADS ** -0.5)
    y_c = attend_dsa(sq, sk, sv, iq, ik, iw).reshape(b, t, DSA_HEADS * HEAD_DIM)

    g = cf[..., :CONF_WIDTH] * jax.nn.sigmoid(cf[..., CONF_WIDTH:])
    c, cf_new = causal_dwconv(g, cf_state, conv_conf)
    y_d = jax.nn.silu(layer_norm(c + conv_conf_bias, conf_ln[0], conf_ln[1]))

    merged = jnp.zeros_like(x)
    for n, y_n in enumerate((y_a, y_b, y_c, y_d)):
        merged = merged + jax.nn.sigmoid(h @ w_gate[n]) * (y_n @ w_branch[n])
    x = x + merged @ w_out

    h2 = rms_norm(x, norm_ffn)
    x = x + (jax.nn.silu(h2 @ w_ffn_gate) * (h2 @ w_ffn_up)) @ w_ffn_down
    return x, sc_new, cf_new, (dk, dv, sk, sv, ik)


def setup_inputs(seed: int = 0) -> dict:
    key = jax.random.key(seed)
    ks = jax.random.split(key, 28)
    f32 = jnp.float32
    n_pages = PAST_LEN // PAGE_SIZE
    n_used = DEC_BATCH * n_pages
    n_pool = n_used + max(1, n_used // 4)
    in_width = sum(IN_SPLIT_SIZES)

    def nrm(k, shape, scale=1.0):
        return jax.random.normal(k, shape, f32) * scale

    page_table = jax.random.permutation(ks[9], n_pool)[:n_used].reshape(DEC_BATCH, n_pages).astype(jnp.int32)
    conf_ln = jnp.stack([1.0 + nrm(ks[18], (DEPTH, CONF_WIDTH), 0.02), nrm(ks[19], (DEPTH, CONF_WIDTH), 0.02)], axis=1)
    return {
        'x_prompt': nrm(ks[0], (BATCH, SEQ, D_MODEL)),
        'x_sample': nrm(ks[1], (DEC_BATCH, DEC_SEQ, D_MODEL)),
        'cache_diff_k': nrm(ks[2], (DEPTH, n_pool, PAGE_SIZE, DIFF_HEADS, 2, HEAD_DIM)),
        'cache_diff_v': nrm(ks[3], (DEPTH, n_pool, PAGE_SIZE, DIFF_HEADS, DIFF_VDIM)),
        'cache_dsa_k': nrm(ks[4], (DEPTH, n_pool, PAGE_SIZE, DSA_HEADS, HEAD_DIM)),
        'cache_dsa_v': nrm(ks[5], (DEPTH, n_pool, PAGE_SIZE, DSA_HEADS, HEAD_DIM)),
        'cache_dsa_kidx': nrm(ks[6], (DEPTH, n_pool, PAGE_SIZE, IDX_DIM)),
        'state_sconv': nrm(ks[7], (DEPTH, DEC_BATCH, SC_CONV_W - 1, SC_WIDTH)),
        'state_cconv': nrm(ks[8], (DEPTH, DEC_BATCH, CONF_CONV_W - 1, CONF_WIDTH), 0.5),
        'page_table': page_table,
        'norm_mix': 1.0 + nrm(ks[10], (DEPTH, D_MODEL), 0.02),
        'w_in': nrm(ks[11], (DEPTH, D_MODEL, in_width), D_MODEL ** -0.5),
        'conv_sc': nrm(ks[12], (DEPTH, SC_CONV_W, SC_WIDTH), SC_CONV_W ** -0.5),
        'qk_norm': 1.0 + nrm(ks[13], (DEPTH, 4, HEAD_DIM), 0.02),
        'diff_lambda': nrm(ks[14], (DEPTH, 4, HEAD_DIM), 0.1),
        'diff_subln': 1.0 + nrm(ks[15], (DEPTH, DIFF_VDIM), 0.02),
        'conv_conf': nrm(ks[16], (DEPTH, CONF_CONV_W, CONF_WIDTH), CONF_CONV_W ** -0.5),
        'conv_conf_bias': nrm(ks[17], (DEPTH, CONF_WIDTH), 0.02),
        'conf_ln': conf_ln,
        'w_gate': nrm(ks[20], (DEPTH, N_BRANCH, D_MODEL, D_MODEL), D_MODEL ** -0.5),
        'w_branch': nrm(ks[21], (DEPTH, N_BRANCH, BRANCH_WIDTH, D_MODEL), BRANCH_WIDTH ** -0.5),
        'w_out': nrm(ks[22], (DEPTH, D_MODEL, D_MODEL), D_MODEL ** -0.5),
        'norm_ffn': 1.0 + nrm(ks[23], (DEPTH, D_MODEL), 0.02),
        'w_ffn_gate': nrm(ks[24], (DEPTH, D_MODEL, FFN_HIDDEN), D_MODEL ** -0.5),
        'w_ffn_up': nrm(ks[25], (DEPTH, D_MODEL, FFN_HIDDEN), D_MODEL ** -0.5),
        'w_ffn_down': nrm(ks[26], (DEPTH, FFN_HIDDEN, D_MODEL), FFN_HIDDEN ** -0.5),
    }


def reference(x_prompt, x_sample, cache_diff_k, cache_diff_v, cache_dsa_k, cache_dsa_v, cache_dsa_kidx,
              state_sconv, state_cconv, page_table, norm_mix, w_in, conv_sc, qk_norm, diff_lambda, diff_subln,
              conv_conf, conv_conf_bias, conf_ln, w_gate, w_branch, w_out, norm_ffn, w_ffn_gate, w_ffn_up,
              w_ffn_down):
    bp, tp, _ = x_prompt.shape
    bs, ts, _ = x_sample.shape
    past = page_table.shape[1] * PAGE_SIZE
    pos_p = jnp.arange(tp, dtype=jnp.int32)
    pos_s = past + jnp.arange(ts, dtype=jnp.int32)
    kpos_s = jnp.arange(past + ts, dtype=jnp.int32)
    topk_p = min(TOPK_MAX, tp // 4)
    topk_s = min(TOPK_MAX, (past + ts) // 4)
    sc0 = jnp.zeros((bp, SC_CONV_W - 1, SC_WIDTH), x_prompt.dtype)
    cf0 = jnp.zeros((bp, CONF_CONV_W - 1, CONF_WIDTH), x_prompt.dtype)

    xp, xs = x_prompt, x_sample
    rows_p, rows_s, sc_p, sc_s, cf_p, cf_s = [], [], [], [], [], []
    for l in range(DEPTH):
        lw = (norm_mix[l], w_in[l], conv_sc[l], qk_norm[l], diff_lambda[l], diff_subln[l], conv_conf[l],
              conv_conf_bias[l], conf_ln[l], w_gate[l], w_branch[l], w_out[l], norm_ffn[l], w_ffn_gate[l],
              w_ffn_up[l], w_ffn_down[l])
        lam_init = 0.8 - 0.6 * math.exp(-0.3 * l)
        attend_diff_p = functools.partial(prompt_diff_attend, pos=pos_p)
        attend_dsa_p = functools.partial(prompt_dsa_attend, pos=pos_p, topk=topk_p)
        xp, scn, cfn, rows = layer_forward(xp, pos_p, sc0, cf0, attend_diff_p, attend_dsa_p, lam_init, *lw)
        rows_p.append(rows)
        sc_p.append(scn)
        cf_p.append(cfn)

        attend_diff_s = functools.partial(sample_diff_attend, cache_k=cache_diff_k[l], cache_v=cache_diff_v[l],
                                          page_table=page_table, q_pos=pos_s, k_pos=kpos_s)
        attend_dsa_s = functools.partial(sample_dsa_attend, cache_k=cache_dsa_k[l], cache_v=cache_dsa_v[l],
                                         cache_kidx=cache_dsa_kidx[l], page_table=page_table, q_pos=pos_s,
                                         k_pos=kpos_s, topk=topk_s)
        xs, scn, cfn, rows = layer_forward(xs, pos_s, state_sconv[l], state_cconv[l], attend_diff_s, attend_dsa_s,
                                           lam_init, *lw)
        rows_s.append(rows)
        sc_s.append(scn)
        cf_s.append(cfn)

    diff_k_p = jnp.stack([r[0] for r in rows_p])
    diff_k_s = jnp.stack([r[0] for r in rows_s])
    diff_v_p = jnp.stack([r[1] for r in rows_p])
    diff_v_s = jnp.stack([r[1] for r in rows_s])
    dsa_k_p = jnp.stack([r[2] for r in rows_p])
    dsa_k_s = jnp.stack([r[2] for r in rows_s])
    dsa_v_p = jnp.stack([r[3] for r in rows_p])
    dsa_v_s = jnp.stack([r[3] for r in rows_s])
    dsa_kidx_p = jnp.stack([r[4] for r in rows_p])
    dsa_kidx_s = jnp.stack([r[4] for r in rows_s])
    sconv_p = jnp.stack(sc_p)
    sconv_s = jnp.stack(sc_s)
    cconv_p = jnp.stack(cf_p)
    cconv_s = jnp.stack(cf_s)
    return (xp, xs, diff_k_p, diff_k_s, diff_v_p, diff_v_s, dsa_k_p, dsa_k_s, dsa_v_p, dsa_v_s,
            dsa_kidx_p, dsa_kidx_s, sconv_p, sconv_s, cconv_p, cconv_s)
```

```python
import functools
import math

import jax
import jax.numpy as jnp
from jax import lax
from jax.experimental import pallas as pl
from jax.experimental.pallas import tpu as pltpu

F32 = jnp.float32
BF16 = jnp.bfloat16

HEAD_DIM = 128
IDX_HEADS = 8
IDX_DIM = 64
TOPK_MAX = 256
PAGE_SIZE = 128
ROPE_THETA = 10000.0
NORM_EPS = 1e-6
LANES = 128
SUBLANES = 8
VMEM_LIMIT = 56 * 1024 * 1024
NEG_BIG = -1e30

(U_SC_IN, U_SC_B, U_SC_C, U_DQ, U_DK, U_DV, U_SQ, U_SK, U_SV, U_IDX, U_CF_A, U_CF_B) = range(12)
N_UBLOCKS = 12


def _cparams(n_axes):
    return pltpu.CompilerParams(dimension_semantics=("arbitrary",) * n_axes,
                                vmem_limit_bytes=VMEM_LIMIT)


def _dot(a, b):
    return jnp.dot(a, b, preferred_element_type=F32)


def _dot_nt(a, b):
    return lax.dot_general(a, b, (((1,), (1,)), ((), ())), preferred_element_type=F32)


def _sigmoid(x):
    return 1.0 / (1.0 + jnp.exp(-x))


def _pick_tile(n, candidates):
    for c in candidates:
        if n % c == 0:
            return c
    return n


def _rmsnorm_kernel(x_ref, g_ref, o_ref):
    x = x_ref[...]
    y = x * lax.rsqrt(jnp.mean(x * x, axis=-1, keepdims=True) + NORM_EPS)
    o_ref[...] = (y * g_ref[...]).astype(o_ref.dtype)


def _rmsnorm(x, g):
    m, d = x.shape
    tm = _pick_tile(m, (256, 128, 64, 32, 16, 8))
    return pl.pallas_call(
        _rmsnorm_kernel,
        grid=(m // tm,),
        in_specs=[pl.BlockSpec((tm, d), lambda i: (i, 0)),
                  pl.BlockSpec((1, d), lambda i: (0, 0))],
        out_specs=pl.BlockSpec((tm, d), lambda i: (i, 0)),
        out_shape=jax.ShapeDtypeStruct((m, d), BF16),
        compiler_params=_cparams(1),
        name="rmsnorm",
    )(x, g.reshape(1, d))


def _mm_kernel(x_ref, w_ref, o_ref):
    o_ref[...] = _dot(x_ref[...], w_ref[...]).astype(o_ref.dtype)


def _matmul(x, w, tm, tn, out_dtype):
    m, k = x.shape
    n = w.shape[1]
    return pl.pallas_call(
        _mm_kernel,
        grid=(m // tm, n // tn),
        in_specs=[pl.BlockSpec((tm, k), lambda i, j: (i, 0)),
                  pl.BlockSpec((k, tn), lambda i, j: (0, j))],
        out_specs=pl.BlockSpec((tm, tn), lambda i, j: (i, j)),
        out_shape=jax.ShapeDtypeStruct((m, n), out_dtype),
        compiler_params=_cparams(2),
        name="in_proj",
    )(x, w)


def _merge_kernel(h_ref, y_ref, wg_ref, wb_ref, o_ref, acc_ref):
    n = pl.program_id(2)
    gate = _sigmoid(_dot(h_ref[...], wg_ref[...]))
    val = gate * _dot(y_ref[...], wb_ref[...])

    @pl.when(n == 0)
    def _():
        acc_ref[...] = val

    @pl.when(n > 0)
    def _():
        acc_ref[...] += val

    @pl.when(n == pl.num_programs(2) - 1)
    def _():
        o_ref[...] = acc_ref[...].astype(o_ref.dtype)


def _merge(h, y, wg, wb, tm, tn):
    m, d = h.shape
    nb, _, bw = y.shape
    return pl.pallas_call(
        _merge_kernel,
        grid=(m // tm, d // tn, nb),
        in_specs=[pl.BlockSpec((tm, d), lambda i, j, n: (i, 0)),
                  pl.BlockSpec((None, tm, bw), lambda i, j, n: (n, i, 0)),
                  pl.BlockSpec((None, d, tn), lambda i, j, n: (n, 0, j)),
                  pl.BlockSpec((None, bw, tn), lambda i, j, n: (n, 0, j))],
        out_specs=pl.BlockSpec((tm, tn), lambda i, j, n: (i, j)),
        out_shape=jax.ShapeDtypeStruct((m, d), BF16),
        scratch_shapes=[pltpu.VMEM((tm, tn), F32)],
        compiler_params=_cparams(3),
        name="gated_merge",
    )(h, y, wg, wb)


def _resid_mm_kernel(x_ref, a_ref, w_ref, o_ref):
    o_ref[...] = x_ref[...] + _dot(a_ref[...], w_ref[...])


def _resid_matmul(x, a, w, tm, tn, name):
    m, k = a.shape
    n = w.shape[1]
    return pl.pallas_call(
        _resid_mm_kernel,
        grid=(m // tm, n // tn),
        in_specs=[pl.BlockSpec((tm, tn), lambda i, j: (i, j)),
                  pl.BlockSpec((tm, k), lambda i, j: (i, 0)),
                  pl.BlockSpec((k, tn), lambda i, j: (0, j))],
        out_specs=pl.BlockSpec((tm, tn), lambda i, j: (i, j)),
        out_shape=jax.ShapeDtypeStruct((m, n), F32),
        compiler_params=_cparams(2),
        name=name,
    )(x, a, w)


def _ffn_up_kernel(h_ref, wg_ref, wu_ref, o_ref):
    h = h_ref[...]
    g = _dot(h, wg_ref[...])
    u = _dot(h, wu_ref[...])
    o_ref[...] = (g * _sigmoid(g) * u).astype(o_ref.dtype)


def _ffn_up(h, wg, wu, tm, tn):
    m, k = h.shape
    n = wg.shape[1]
    return pl.pallas_call(
        _ffn_up_kernel,
        grid=(m // tm, n // tn),
        in_specs=[pl.BlockSpec((tm, k), lambda i, j: (i, 0)),
                  pl.BlockSpec((k, tn), lambda i, j: (0, j)),
                  pl.BlockSpec((k, tn), lambda i, j: (0, j))],
        out_specs=pl.BlockSpec((tm, tn), lambda i, j: (i, j)),
        out_shape=jax.ShapeDtypeStruct((m, n), BF16),
        compiler_params=_cparams(2),
        name="ffn_up",
    )(h, wg, wu)


def _sconv_kernel(cin_ref, cb_ref, cc_ref, st_ref, w_ref, y_ref, ns_ref, ext_ref, *, tt, width):
    t = pl.program_id(1)
    hist = width - 1
    base = SUBLANES

    @pl.when(t == 0)
    def _():
        ext_ref[base - hist:base, :] = st_ref[...]

    @pl.when(t > 0)
    def _():
        ext_ref[base - hist:base, :] = ext_ref[base + tt - hist:base + tt, :]

    ext_ref[base:base + tt, :] = cc_ref[...] * cin_ref[...]
    z = None
    for k in range(width):
        term = w_ref[k:k + 1, :] * ext_ref[base - hist + k:base - hist + k + tt, :]
        z = term if z is None else z + term
    y_ref[...] = (cb_ref[...] * z).astype(y_ref.dtype)

    @pl.when(t == pl.num_programs(1) - 1)
    def _():
        ns_ref[...] = ext_ref[base + tt - hist:base + tt, :]


def _sconv(u3, state, w, tt):
    b, t, _ = u3.shape
    width, bw = w.shape
    assert width - 1 <= SUBLANES and (tt >= width - 1 or t == tt)

    def ublock(c):
        return pl.BlockSpec((None, tt, bw), lambda bi, ti, c=c: (bi, ti, c))

    return pl.pallas_call(
        functools.partial(_sconv_kernel, tt=tt, width=width),
        grid=(b, t // tt),
        in_specs=[ublock(U_SC_IN), ublock(U_SC_B), ublock(U_SC_C),
                  pl.BlockSpec((None, width - 1, bw), lambda bi, ti: (bi, 0, 0)),
                  pl.BlockSpec((width, bw), lambda bi, ti: (0, 0))],
        out_specs=[pl.BlockSpec((None, tt, bw), lambda bi, ti: (bi, ti, 0)),
                   pl.BlockSpec((None, width - 1, bw), lambda bi, ti: (bi, 0, 0))],
        out_shape=[jax.ShapeDtypeStruct((b, t, bw), BF16),
                   jax.ShapeDtypeStruct((b, width - 1, bw), F32)],
        scratch_shapes=[pltpu.VMEM((SUBLANES + tt, bw), F32)],
        compiler_params=_cparams(2),
        name="short_conv",
    )(u3, u3, u3, state, w)


def _cconv_kernel(ca_ref, cb_ref, st_ref, w_ref, bias_ref, ln_ref, y_ref, ns_ref, ext_ref, c_ref,
                  *, tt, width, base):
    t = pl.program_id(1)
    hist = width - 1
    bw = ca_ref.shape[-1]

    @pl.when(t == 0)
    def _():
        ext_ref[base - hist:base, :] = st_ref[...]

    @pl.when(t > 0)
    def _():
        ext_ref[base - hist:base, :] = ext_ref[base + tt - hist:base + tt, :]

    ext_ref[base:base + tt, :] = ca_ref[...] * _sigmoid(cb_ref[...])
    for j in range(bw // LANES):
        ls = slice(j * LANES, (j + 1) * LANES)
        acc = None
        for k in range(width):
            term = w_ref[k:k + 1, ls] * ext_ref[base - hist + k:base - hist + k + tt, ls]
            acc = term if acc is None else acc + term
        c_ref[:, ls] = acc + bias_ref[:, ls]
    c = c_ref[...]
    xc = c - jnp.mean(c, axis=-1, keepdims=True)
    yn = xc * lax.rsqrt(jnp.mean(xc * xc, axis=-1, keepdims=True) + NORM_EPS)
    yn = yn * ln_ref[0:1, :] + ln_ref[1:2, :]
    y_ref[...] = (yn * _sigmoid(yn)).astype(y_ref.dtype)

    @pl.when(t == pl.num_programs(1) - 1)
    def _():
        ns_ref[...] = ext_ref[base + tt - hist:base + tt, :]


def _cconv(u3, state, w, bias, ln, tt):
    b, t, _ = u3.shape
    width, bw = w.shape
    hist = width - 1
    base = -(-hist // SUBLANES) * SUBLANES
    assert tt >= hist or t == tt

    def ublock(c):
        return pl.BlockSpec((None, tt, bw), lambda bi, ti, c=c: (bi, ti, c))

    return pl.pallas_call(
        functools.partial(_cconv_kernel, tt=tt, width=width, base=base),
        grid=(b, t // tt),
        in_specs=[ublock(U_CF_A), ublock(U_CF_B),
                  pl.BlockSpec((None, hist, bw), lambda bi, ti: (bi, 0, 0)),
                  pl.BlockSpec((width, bw), lambda bi, ti: (0, 0)),
                  pl.BlockSpec((1, bw), lambda bi, ti: (0, 0)),
                  pl.BlockSpec((2, bw), lambda bi, ti: (0, 0))],
        out_specs=[pl.BlockSpec((None, tt, bw), lambda bi, ti: (bi, ti, 0)),
                   pl.BlockSpec((None, hist, bw), lambda bi, ti: (bi, 0, 0))],
        out_shape=[jax.ShapeDtypeStruct((b, t, bw), BF16),
                   jax.ShapeDtypeStruct((b, hist, bw), F32)],
        scratch_shapes=[pltpu.VMEM((base + tt, bw), F32), pltpu.VMEM((tt, bw), F32)],
        compiler_params=_cparams(2),
        name="conformer_conv",
    )(u3, u3, state, w, bias.reshape(1, bw), ln)


def _qkprep_kernel(dq_ref, dk_ref, dv_ref, sq_ref, sk_ref, sv_ref, idx_ref, g_ref,
                   cos_ref, sin_ref, cos64_ref, sina_ref, sinb_ref,
                   dqb_ref, dkf_ref, dkb_ref, dvb_ref, sqb_ref, skf_ref, skb_ref, svb_ref,
                   iqb_ref, ikw_ref, iklo_ref, ikhi_ref):
    cos = cos_ref[...]
    sin = sin_ref[...]
    n_heads = dq_ref.shape[-1] // HEAD_DIM

    def norm_rope(x, g):
        y = x * lax.rsqrt(jnp.mean(x * x, axis=-1, keepdims=True) + NORM_EPS) * g
        return y * cos + pltpu.roll(y, HEAD_DIM // 2, 1) * sin

    for h in range(n_heads):
        ls = slice(h * HEAD_DIM, (h + 1) * HEAD_DIM)
        dqb_ref[:, ls] = norm_rope(dq_ref[:, ls], g_ref[0:1, :]).astype(BF16)
        dk = norm_rope(dk_ref[:, ls], g_ref[1:2, :])
        dkf_ref[:, ls] = dk
        dkb_ref[:, ls] = dk.astype(BF16)
        sqb_ref[:, ls] = norm_rope(sq_ref[:, ls], g_ref[2:3, :]).astype(BF16)
        sk = norm_rope(sk_ref[:, ls], g_ref[3:4, :])
        skf_ref[:, ls] = sk
        skb_ref[:, ls] = sk.astype(BF16)
    dvb_ref[...] = dv_ref[...].astype(BF16)
    svb_ref[...] = sv_ref[...].astype(BF16)

    cos64 = cos64_ref[...]
    sina = sina_ref[...]
    sinb = sinb_ref[...]

    def rope64(x):
        return (x * cos64 + pltpu.roll(x, LANES - IDX_DIM // 2, 1) * sina
                + pltpu.roll(x, IDX_DIM // 2, 1) * sinb)

    n_iq = IDX_HEADS * IDX_DIM
    for j in range(n_iq // LANES):
        ls = slice(j * LANES, (j + 1) * LANES)
        iqb_ref[:, ls] = rope64(idx_ref[:, ls]).astype(BF16)
    xs = idx_ref[:, n_iq:n_iq + LANES]
    lane = lax.broadcasted_iota(jnp.int32, xs.shape, 1)
    ik = jnp.where(lane < IDX_DIM, rope64(xs), 0.0)
    iw = jnp.where((lane >= IDX_DIM) & (lane < IDX_DIM + IDX_HEADS), xs * (IDX_HEADS ** -0.5), 0.0)
    ikw_ref[...] = ik + iw
    iklo_ref[...] = ik.astype(BF16)
    ikhi_ref[...] = pltpu.roll(ik, IDX_DIM, 1).astype(BF16)


def _qkprep(u, qk_norm, tabs, tt, n_tblocks):
    m = u.shape[0]
    bw = u.shape[1] // N_UBLOCKS

    def ublock(c):
        return pl.BlockSpec((tt, bw), lambda i, c=c: (i, c))

    def tab():
        return pl.BlockSpec((tt, LANES), lambda i: (i % n_tblocks, 0))

    def out(w):
        return pl.BlockSpec((tt, w), lambda i: (i, 0))

    n_iq = IDX_HEADS * IDX_DIM
    shapes = [(bw, BF16), (bw, F32), (bw, BF16), (bw, BF16), (bw, BF16), (bw, F32), (bw, BF16), (bw, BF16),
              (n_iq, BF16), (LANES, F32), (LANES, BF16), (LANES, BF16)]
    return pl.pallas_call(
        _qkprep_kernel,
        grid=(m // tt,),
        in_specs=[ublock(U_DQ), ublock(U_DK), ublock(U_DV), ublock(U_SQ), ublock(U_SK), ublock(U_SV),
                  ublock(U_IDX), pl.BlockSpec((4, HEAD_DIM), lambda i: (0, 0)),
                  tab(), tab(), tab(), tab(), tab()],
        out_specs=[out(w) for w, _ in shapes],
        out_shape=[jax.ShapeDtypeStruct((m, w), dt) for w, dt in shapes],
        compiler_params=_cparams(1),
        name="qk_norm_rope",
    )(u, u, u, u, u, u, u, qk_norm, *tabs)


def _rope_tables(pos):
    def angles(d):
        inv = ROPE_THETA ** (-jnp.arange(0, d, 2, dtype=F32) / d)
        return pos.astype(F32)[:, None] * inv[None, :]

    a = angles(HEAD_DIM)
    cos = jnp.concatenate([jnp.cos(a), jnp.cos(a)], axis=1)
    sin = jnp.concatenate([-jnp.sin(a), jnp.sin(a)], axis=1)
    a2 = angles(IDX_DIM)
    c2, s2, z2 = jnp.cos(a2), jnp.sin(a2), jnp.zeros_like(a2)
    cos64 = jnp.concatenate([c2, c2, c2, c2], axis=1)
    sina = jnp.concatenate([-s2, z2, -s2, z2], axis=1)
    sinb = jnp.concatenate([z2, s2, z2, s2], axis=1)
    return (cos, sin, cos64, sina, sinb)


def _diff_lambda(dl_ref, lam_init):
    lv = dl_ref[...]
    a = jnp.sum(lv[0:1, :] * lv[1:2, :], axis=-1, keepdims=True)
    b = jnp.sum(lv[2:3, :] * lv[3:4, :], axis=-1, keepdims=True)
    return jnp.exp(a) - jnp.exp(b) + lam_init


def _subln(o, g, lam_init):
    y = o * lax.rsqrt(jnp.mean(o * o, axis=-1, keepdims=True) + NORM_EPS)
    return (y * g) * (1.0 - lam_init)


def _topk_select(score, col, k, idx_bits):
    r = score.shape[0]
    bits = pltpu.bitcast(score + 0.0, jnp.int32)
    key = jnp.where(bits < 0, bits ^ jnp.int32(0x7FFFFFFF), bits)
    kf = float(k)

    def count(pred):
        return jnp.sum(jnp.where(pred, 1.0, 0.0), axis=-1, keepdims=True)

    zero = jnp.zeros((r, 1), jnp.int32)
    start = jnp.where(count(key >= zero) >= kf, zero, jnp.full((r, 1), -2 ** 31, jnp.int32))

    def value_step(i, cur):
        cand = cur + jnp.left_shift(jnp.int32(1), 30 - i)
        return jnp.where(count(key >= cand) >= kf, cand, cur)

    thr = lax.fori_loop(0, 31, value_step, start)
    above = key > thr
    tie = key == thr
    need = kf - count(above)

    def index_step(i, cur):
        cand = cur + jnp.left_shift(jnp.int32(1), idx_bits - 1 - i)
        cnt = jnp.sum(jnp.where(tie, jnp.where(col < cand, 1.0, 0.0), 0.0), axis=-1, keepdims=True)
        return jnp.where(cnt <= need, cand, cur)

    bound = lax.fori_loop(0, idx_bits, index_step, zero)
    return above | (tie & (col < bound))


def _diff_attn_kernel(q_ref, k_ref, v_ref, dl_ref, g_ref, o_ref, *, tq, lam_init):
    qi = pl.program_id(2)
    q = q_ref[...]
    k = k_ref[...]
    t = k.shape[0]
    row = qi * tq + lax.broadcasted_iota(jnp.int32, (tq, t), 0)
    col = lax.broadcasted_iota(jnp.int32, (tq, t), 1)
    mask = col <= row
    scale = HEAD_DIM ** -0.5

    def softmax(c):
        ls = slice(c * HEAD_DIM, (c + 1) * HEAD_DIM)
        s = jnp.where(mask, _dot_nt(q[:, ls], k[:, ls]) * scale, -jnp.inf)
        e = jnp.exp(s - jnp.max(s, axis=-1, keepdims=True))
        return e * (1.0 / jnp.sum(e, axis=-1, keepdims=True))

    lam = _diff_lambda(dl_ref, lam_init)
    p = softmax(0) - lam * softmax(1)
    o = _dot(p.astype(BF16), v_ref[...])
    o_ref[...] = _subln(o, g_ref[...], lam_init).astype(o_ref.dtype)


def _diff_attn_prompt(dq, dk, dv, diff_lambda, subln, b, t, lam_init, tq):
    bw = dq.shape[-1]
    vd = 2 * HEAD_DIM
    nh = bw // vd
    q3, k3, v3 = (a.reshape(b, t, bw) for a in (dq, dk, dv))
    return pl.pallas_call(
        functools.partial(_diff_attn_kernel, tq=tq, lam_init=lam_init),
        grid=(b, nh, t // tq),
        in_specs=[pl.BlockSpec((None, tq, vd), lambda bi, h, qi: (bi, qi, h)),
                  pl.BlockSpec((None, t, vd), lambda bi, h, qi: (bi, 0, h)),
                  pl.BlockSpec((None, t, vd), lambda bi, h, qi: (bi, 0, h)),
                  pl.BlockSpec((4, HEAD_DIM), lambda bi, h, qi: (0, 0)),
                  pl.BlockSpec((1, vd), lambda bi, h, qi: (0, 0))],
        out_specs=pl.BlockSpec((None, tq, vd), lambda bi, h, qi: (bi, qi, h)),
        out_shape=jax.ShapeDtypeStruct((b, t, bw), BF16),
        compiler_params=_cparams(3),
        name="diff_attn_prompt",
    )(q3, k3, v3, diff_lambda, subln.reshape(1, vd))


def _dsa_attn_kernel(q_ref, k_ref, v_ref, iq_ref, iklo_ref, ikhi_ref, iw_ref, o_ref, *, tq, topk, idx_bits):
    qi = pl.program_id(1)
    t = k_ref.shape[0]
    iklo = iklo_ref[...]
    ikhi = ikhi_ref[...]
    iw = iw_ref[...]
    score = None
    for j in range(IDX_HEADS // 2):
        qpair = iq_ref[:, j * LANES:(j + 1) * LANES]
        for half, ik in enumerate((iklo, ikhi)):
            h = 2 * j + half
            sc = jnp.maximum(_dot_nt(qpair, ik) * (IDX_DIM ** -0.5), 0.0)
            term = iw[:, IDX_DIM + h:IDX_DIM + h + 1] * sc
            score = term if score is None else score + term
    row = qi * tq + lax.broadcasted_iota(jnp.int32, (tq, t), 0)
    col = lax.broadcasted_iota(jnp.int32, (tq, t), 1)
    causal = col <= row
    sel = _topk_select(jnp.where(causal, score, -jnp.inf), col, topk, idx_bits)
    keep = causal & sel
    scale = HEAD_DIM ** -0.5
    for h in range(q_ref.shape[-1] // HEAD_DIM):
        ls = slice(h * HEAD_DIM, (h + 1) * HEAD_DIM)
        s = jnp.where(keep, _dot_nt(q_ref[:, ls], k_ref[:, ls]) * scale, -jnp.inf)
        e = jnp.exp(s - jnp.max(s, axis=-1, keepdims=True))
        a = e * (1.0 / jnp.sum(e, axis=-1, keepdims=True))
        o_ref[:, ls] = _dot(a.astype(BF16), v_ref[:, ls]).astype(o_ref.dtype)


def _dsa_attn_prompt(sq, sk, sv, iq, iklo, ikhi, ikw, b, t, tq):
    bw = sq.shape[-1]
    n_iq = iq.shape[-1]
    topk = min(TOPK_MAX, t // 4)
    idx_bits = int(t).bit_length()

    def qblock(w):
        return pl.BlockSpec((None, tq, w), lambda bi, qi: (bi, qi, 0))

    def kblock(w):
        return pl.BlockSpec((None, t, w), lambda bi, qi: (bi, 0, 0))

    return pl.pallas_call(
        functools.partial(_dsa_attn_kernel, tq=tq, topk=topk, idx_bits=idx_bits),
        grid=(b, t // tq),
        in_specs=[qblock(bw), kblock(bw), kblock(bw), qblock(n_iq), kblock(LANES), kblock(LANES),
                  qblock(LANES)],
        out_specs=qblock(bw),
        out_shape=jax.ShapeDtypeStruct((b, t, bw), BF16),
        compiler_params=_cparams(2),
        name="dsa_attn_prompt",
    )(sq.reshape(b, t, bw), sk.reshape(b, t, bw), sv.reshape(b, t, bw), iq.reshape(b, t, n_iq),
      iklo.reshape(b, t, LANES), ikhi.reshape(b, t, LANES), ikw.reshape(b, t, LANES))


def _pages_per_step(n_pages):
    return _pick_tile(n_pages, (8, 4, 2, 1))


def _page_specs(layer, npp, n_steps, width):
    def spec(r):
        def index(bi, p, pt):
            return (layer, pt[bi, jnp.minimum(p, n_steps - 1) * npp + r], 0, 0)
        return pl.BlockSpec((None, None, PAGE_SIZE, width), index)
    return [spec(r) for r in range(npp)]


def _online_softmax_step(s, keep, v_parts, m_ref, l_ref, acc_ref):
    s = jnp.where(keep, s, NEG_BIG)
    m_old = m_ref[...]
    m_new = jnp.maximum(m_old, jnp.max(s, axis=-1, keepdims=True))
    alpha = jnp.exp(m_old - m_new)
    p = jnp.where(keep, jnp.exp(s - m_new), 0.0)
    l_ref[...] = alpha * l_ref[...] + jnp.sum(p, axis=-1, keepdims=True)
    pv = None
    for r, v in enumerate(v_parts):
        part = _dot(p[:, r * PAGE_SIZE:(r + 1) * PAGE_SIZE].astype(BF16), v)
        pv = part if pv is None else pv + part
    acc_ref[...] = alpha * acc_ref[...] + pv
    m_ref[...] = m_new


def _diff_attn_sample_kernel(pt_ref, q_ref, *refs, npp, n_steps, t_new, lam_init):
    k_refs = refs[:npp]
    v_refs = refs[npp:2 * npp]
    kn_ref, vn_ref, dl_ref, g_ref, o_ref, m_ref, l_ref, acc_ref = refs[2 * npp:]
    p = pl.program_id(1)
    rows = q_ref.shape[0]
    scale = HEAD_DIM ** -0.5

    @pl.when(p == 0)
    def _():
        m_ref[...] = jnp.full(m_ref.shape, NEG_BIG, F32)
        l_ref[...] = jnp.zeros(l_ref.shape, F32)
        acc_ref[...] = jnp.zeros(acc_ref.shape, F32)

    @pl.when(p < n_steps)
    def _():
        q = q_ref[...]
        s = jnp.concatenate([_dot_nt(q, kr[...].astype(BF16)) for kr in k_refs], axis=1) * scale
        keep = jnp.full(s.shape, True)
        _online_softmax_step(s, keep, [vr[...].astype(BF16) for vr in v_refs], m_ref, l_ref, acc_ref)

    @pl.when(p == n_steps)
    def _():
        s = _dot_nt(q_ref[...], kn_ref[...]) * scale
        qrow = lax.broadcasted_iota(jnp.int32, s.shape, 0) % SUBLANES
        col = lax.broadcasted_iota(jnp.int32, s.shape, 1)
        keep = (col <= qrow) & (col < t_new)
        _online_softmax_step(s, keep, [vn_ref[...]], m_ref, l_ref, acc_ref)
        out = acc_ref[...] * (1.0 / l_ref[...])
        lam = _diff_lambda(dl_ref, lam_init)
        vd = 2 * HEAD_DIM
        for h in range(rows // (2 * SUBLANES)):
            r0 = h * 2 * SUBLANES
            o0 = out[r0:r0 + SUBLANES, h * vd:(h + 1) * vd]
            o1 = out[r0 + SUBLANES:r0 + 2 * SUBLANES, h * vd:(h + 1) * vd]
            o_ref[:, h * vd:(h + 1) * vd] = _subln(o0 - lam * o1, g_ref[...], lam_init).astype(o_ref.dtype)


def _block_diag_queries(q, n_groups):
    b, t, _ = q.shape
    qg = q.reshape(b, t, n_groups, HEAD_DIM).transpose(0, 2, 1, 3)
    qg = jnp.pad(qg, ((0, 0), (0, 0), (0, SUBLANES - t), (0, 0)))
    eye = jnp.eye(n_groups, dtype=q.dtype)
    out = qg[:, :, :, None, :] * eye[None, :, None, :, None]
    return out.reshape(b, n_groups * SUBLANES, n_groups * HEAD_DIM)


def _pad_rows(a, rows):
    return jnp.pad(a, ((0, 0), (0, rows - a.shape[1]), (0, 0)))


def _diff_attn_sample(dq, dk, dv, cache_k, cache_v, layer, page_table, diff_lambda, subln, b, t, lam_init):
    bw = dq.shape[-1]
    vd = 2 * HEAD_DIM
    n_pages = page_table.shape[1]
    npp = _pages_per_step(n_pages)
    n_steps = n_pages // npp
    n_groups = bw // HEAD_DIM
    rows = n_groups * SUBLANES
    qbd = _block_diag_queries(dq.reshape(b, t, bw), n_groups)
    kn = _pad_rows(dk.reshape(b, t, bw), PAGE_SIZE)
    vn = _pad_rows(dv.reshape(b, t, bw), PAGE_SIZE)
    ck = cache_k.reshape(cache_k.shape[0], cache_k.shape[1], PAGE_SIZE, bw)
    cv = cache_v.reshape(cache_v.shape[0], cache_v.shape[1], PAGE_SIZE, bw)

    def per_batch(r, w):
        return pl.BlockSpec((None, r, w), lambda bi, p, pt: (bi, 0, 0))

    grid_spec = pltpu.PrefetchScalarGridSpec(
        num_scalar_prefetch=1,
        grid=(b, n_steps + 1),
        in_specs=[per_batch(rows, bw)]
        + _page_specs(layer, npp, n_steps, bw) + _page_specs(layer, npp, n_steps, bw)
        + [per_batch(PAGE_SIZE, bw), per_batch(PAGE_SIZE, bw),
           pl.BlockSpec((4, HEAD_DIM), lambda bi, p, pt: (0, 0)),
           pl.BlockSpec((1, vd), lambda bi, p, pt: (0, 0))],
        out_specs=per_batch(SUBLANES, bw),
        scratch_shapes=[pltpu.VMEM((rows, 1), F32), pltpu.VMEM((rows, 1), F32), pltpu.VMEM((rows, bw), F32)],
    )
    out = pl.pallas_call(
        functools.partial(_diff_attn_sample_kernel, npp=npp, n_steps=n_steps, t_new=t, lam_init=lam_init),
        grid_spec=grid_spec,
        out_shape=jax.ShapeDtypeStruct((b, SUBLANES, bw), BF16),
        compiler_params=_cparams(2),
        name="diff_attn_sample",
    )(page_table, qbd, *([ck] * npp), *([cv] * npp), kn, vn, diff_lambda, subln.reshape(1, vd))
    return out[:, :t].reshape(b * t, bw)


def _idx_score_kernel(pt_ref, q_ref, w_ref, *refs, npp, n_steps):
    k_refs = refs[:npp]
    kn_ref, o_ref = refs[npp:]
    p = pl.program_id(1)
    q = q_ref[...]
    w = w_ref[...]

    def scores(kk):
        sc = jnp.maximum(_dot_nt(q, kk) * (IDX_DIM ** -0.5), 0.0) * w
        out = sc[0:SUBLANES]
        for h in range(1, IDX_HEADS):
            out = out + sc[h * SUBLANES:(h + 1) * SUBLANES]
        return out

    @pl.when(p < n_steps)
    def _():
        for r, kr in enumerate(k_refs):
            o_ref[:, r * PAGE_SIZE:(r + 1) * PAGE_SIZE] = scores(kr[...].astype(BF16))

    @pl.when(p == n_steps)
    def _():
        o_ref[...] = scores(kn_ref[...])


def _select_sample_kernel(s_ref, o_ref, *, past, t_new, topk, idx_bits):
    s = s_ref[...]
    qrow = jnp.minimum(lax.broadcasted_iota(jnp.int32, s.shape, 0), t_new - 1)
    col = lax.broadcasted_iota(jnp.int32, s.shape, 1)
    admissible = (col <= past + qrow) & (col < past + t_new)
    sel = _topk_select(jnp.where(admissible, s, -jnp.inf), col, topk, idx_bits)
    o_ref[...] = jnp.where(admissible, jnp.where(sel, 1.0, 0.0), 0.0)


def _dsa_attn_sample_kernel(pt_ref, q_ref, keep_ref, *refs, npp, n_steps):
    k_refs = refs[:npp]
    v_refs = refs[npp:2 * npp]
    kn_ref, vn_ref, o_ref, m_ref, l_ref, acc_ref = refs[2 * npp:]
    p = pl.program_id(1)
    rows = q_ref.shape[0]
    scale = HEAD_DIM ** -0.5

    @pl.when(p == 0)
    def _():
        m_ref[...] = jnp.full(m_ref.shape, NEG_BIG, F32)
        l_ref[...] = jnp.zeros(l_ref.shape, F32)
        acc_ref[...] = jnp.zeros(acc_ref.shape, F32)

    def tiled_keep(n):
        kp = keep_ref[:, 0:n] > 0.5
        return jnp.concatenate([kp] * (rows // SUBLANES), axis=0)

    @pl.when(p < n_steps)
    def _():
        q = q_ref[...]
        s = jnp.concatenate([_dot_nt(q, kr[...].astype(BF16)) for kr in k_refs], axis=1) * scale
        _online_softmax_step(s, tiled_keep(npp * PAGE_SIZE), [vr[...].astype(BF16) for vr in v_refs],
                             m_ref, l_ref, acc_ref)

    @pl.when(p == n_steps)
    def _():
        s = _dot_nt(q_ref[...], kn_ref[...]) * scale
        _online_softmax_step(s, tiled_keep(PAGE_SIZE), [vn_ref[...]], m_ref, l_ref, acc_ref)
        out = acc_ref[...] * (1.0 / l_ref[...])
        for h in range(rows // SUBLANES):
            ls = slice(h * HEAD_DIM, (h + 1) * HEAD_DIM)
            o_ref[:, ls] = out[h * SUBLANES:(h + 1) * SUBLANES, ls].astype(o_ref.dtype)


def _dsa_attn_sample(sq, sk, sv, iq, ikb, ikw, cache_k, cache_v, cache_kidx, layer, page_table, b, t):
    bw = sq.shape[-1]
    n_pages = page_table.shape[1]
    past = n_pages * PAGE_SIZE
    npp = _pages_per_step(n_pages)
    n_steps = n_pages // npp
    step_w = npp * PAGE_SIZE
    s_pad = (n_steps + 1) * step_w
    topk = min(TOPK_MAX, (past + t) // 4)
    idx_bits = int(s_pad).bit_length()

    def per_batch(r, w):
        return pl.BlockSpec((None, r, w), lambda bi, p, pt: (bi, 0, 0))

    n_iq = iq.shape[-1]
    qi = iq.reshape(b, t, IDX_HEADS, IDX_DIM).transpose(0, 2, 1, 3)
    qi = jnp.pad(qi, ((0, 0), (0, 0), (0, SUBLANES - t), (0, 0))).reshape(b, IDX_HEADS * SUBLANES, IDX_DIM)
    wi = ikw.reshape(b, t, LANES)[:, :, IDX_DIM:IDX_DIM + IDX_HEADS].transpose(0, 2, 1)
    wi = jnp.pad(wi, ((0, 0), (0, 0), (0, SUBLANES - t))).reshape(b, IDX_HEADS * SUBLANES, 1)
    kin = _pad_rows(ikb.reshape(b, t, LANES)[:, :, :IDX_DIM], step_w)
    scores = pl.pallas_call(
        functools.partial(_idx_score_kernel, npp=npp, n_steps=n_steps),
        grid_spec=pltpu.PrefetchScalarGridSpec(
            num_scalar_prefetch=1,
            grid=(b, n_steps + 1),
            in_specs=[per_batch(IDX_HEADS * SUBLANES, IDX_DIM), per_batch(IDX_HEADS * SUBLANES, 1)]
            + _page_specs(layer, npp, n_steps, IDX_DIM) + [per_batch(step_w, IDX_DIM)],
            out_specs=pl.BlockSpec((None, SUBLANES, step_w), lambda bi, p, pt: (bi, 0, p)),
        ),
        out_shape=jax.ShapeDtypeStruct((b, SUBLANES, s_pad), F32),
        compiler_params=_cparams(2),
        name="idx_score_sample",
    )(page_table, qi, wi, *([cache_kidx] * npp), kin)

    keep = pl.pallas_call(
        functools.partial(_select_sample_kernel, past=past, t_new=t, topk=topk, idx_bits=idx_bits),
        grid=(b,),
        in_specs=[pl.BlockSpec((None, SUBLANES, s_pad), lambda bi: (bi, 0, 0))],
        out_specs=pl.BlockSpec((None, SUBLANES, s_pad), lambda bi: (bi, 0, 0)),
        out_shape=jax.ShapeDtypeStruct((b, SUBLANES, s_pad), F32),
        compiler_params=_cparams(1),
        name="topk_select_sample",
    )(scores)

    n_groups = bw // HEAD_DIM
    rows = n_groups * SUBLANES
    qbd = _block_diag_queries(sq.reshape(b, t, bw), n_groups)
    kn = _pad_rows(sk.reshape(b, t, bw), PAGE_SIZE)
    vn = _pad_rows(sv.reshape(b, t, bw), PAGE_SIZE)
    ck = cache_k.reshape(cache_k.shape[0], cache_k.shape[1], PAGE_SIZE, bw)
    cv = cache_v.reshape(cache_v.shape[0], cache_v.shape[1], PAGE_SIZE, bw)
    out = pl.pallas_call(
        functools.partial(_dsa_attn_sample_kernel, npp=npp, n_steps=n_steps),
        grid_spec=pltpu.PrefetchScalarGridSpec(
            num_scalar_prefetch=1,
            grid=(b, n_steps + 1),
            in_specs=[per_batch(rows, bw),
                      pl.BlockSpec((None, SUBLANES, step_w), lambda bi, p, pt: (bi, 0, p))]
            + _page_specs(layer, npp, n_steps, bw) + _page_specs(layer, npp, n_steps, bw)
            + [per_batch(PAGE_SIZE, bw), per_batch(PAGE_SIZE, bw)],
            out_specs=per_batch(SUBLANES, bw),
            scratch_shapes=[pltpu.VMEM((rows, 1), F32), pltpu.VMEM((rows, 1), F32),
                            pltpu.VMEM((rows, bw), F32)],
        ),
        out_shape=jax.ShapeDtypeStruct((b, SUBLANES, bw), BF16),
        compiler_params=_cparams(2),
        name="dsa_attn_sample",
    )(page_table, qbd, keep, *([ck] * npp), *([cv] * npp), kn, vn)
    return out[:, :t].reshape(b * t, bw)


def _prep_w_in(w, bw):
    n_idx = IDX_HEADS * IDX_DIM + IDX_DIM + IDX_HEADS
    wb = w.astype(BF16)
    main = wb[:, :9 * bw]
    idx = jnp.pad(wb[:, 9 * bw:9 * bw + n_idx], ((0, 0), (0, bw - n_idx)))
    return jnp.concatenate([main, idx, wb[:, 9 * bw + n_idx:]], axis=1)


def _layer(x, b, t, pos, sc_state, cf_state, attend_diff, attend_dsa, w, tiles):
    m, d = x.shape
    bw = d // 4
    h = _rmsnorm(x, w["norm_mix"])
    u = _matmul(h, w["w_in"], tiles["tm"], tiles["tn_in"], F32)
    u3 = u.reshape(b, t, N_UBLOCKS * bw)

    y_a, sc_new = _sconv(u3, sc_state, w["conv_sc"], tiles["tt_conv"])
    y_d, cf_new = _cconv(u3, cf_state, w["conv_conf"], w["conv_conf_bias"], w["conf_ln"], tiles["tt_conv"])

    tabs = _rope_tables(pos)
    if tiles["tt_prep"] > t:
        tabs = tuple(jnp.tile(a, (tiles["tt_prep"] // t, 1)) for a in tabs)
    n_tblocks = max(1, t // tiles["tt_prep"])
    (dqb, dkf, dkb, dvb, sqb, skf, skb, svb, iqb, ikw, iklo, ikhi) = _qkprep(
        u, w["qk_norm"], tabs, tiles["tt_prep"], n_tblocks)

    y_b = attend_diff(dqb, dkb, dvb)
    y_c = attend_dsa(sqb, skb, svb, iqb, iklo, ikhi, ikw)

    y = jnp.stack([y_a.reshape(m, bw), y_b.reshape(m, bw), y_c.reshape(m, bw), y_d.reshape(m, bw)])
    merged = _merge(h, y, w["w_gate"], w["w_branch"], tiles["tm"], tiles["tn_merge"])
    x = _resid_matmul(x, merged, w["w_out"], tiles["tm"], tiles["tn_out"], "out_proj")

    h2 = _rmsnorm(x, w["norm_ffn"])
    a = _ffn_up(h2, w["w_ffn_gate"], w["w_ffn_up"], tiles["tm"], tiles["tn_ffn"])
    x = _resid_matmul(x, a, w["w_ffn_down"], tiles["tm_down"], tiles["tn_down"], "ffn_down")

    dv = u3[:, :, U_DV * bw:(U_DV + 1) * bw]
    sv = u3[:, :, U_SV * bw:(U_SV + 1) * bw]
    rows = (dkf.reshape(b, t, bw // (2 * HEAD_DIM), 2, HEAD_DIM),
            dv.reshape(b, t, bw // (2 * HEAD_DIM), 2 * HEAD_DIM),
            skf.reshape(b, t, bw // HEAD_DIM, HEAD_DIM),
            sv.reshape(b, t, bw // HEAD_DIM, HEAD_DIM),
            ikw.reshape(b, t, LANES)[:, :, :IDX_DIM])
    return x, sc_new, cf_new, rows


def _tiles(m, t, ffn):
    if m >= 1024:
        tm = _pick_tile(m, (1024, 512, 256))
        return dict(tm=tm, tn_in=1024, tn_merge=512, tn_out=1024, tn_ffn=_pick_tile(ffn, (512, 256, 128)),
                    tm_down=_pick_tile(m, (512, 256)), tn_down=256,
                    tt_conv=_pick_tile(t, (128,)), tt_prep=_pick_tile(t, (256, 128)))
    return dict(tm=m, tn_in=1024, tn_merge=512, tn_out=1024, tn_ffn=_pick_tile(ffn, (512, 256, 128)),
                tm_down=m, tn_down=512, tt_conv=t, tt_prep=m)


def kernel(x_prompt, x_sample, cache_diff_k, cache_diff_v, cache_dsa_k, cache_dsa_v, cache_dsa_kidx,
           state_sconv, state_cconv, page_table, norm_mix, w_in, conv_sc, qk_norm, diff_lambda, diff_subln,
           conv_conf, conv_conf_bias, conf_ln, w_gate, w_branch, w_out, norm_ffn, w_ffn_gate, w_ffn_up,
           w_ffn_down):
    bp, tp, d = x_prompt.shape
    bs, ts, _ = x_sample.shape
    depth = w_in.shape[0]
    bw = d // 4
    ffn = w_ffn_gate.shape[-1]
    past = page_table.shape[1] * PAGE_SIZE
    pos_p = jnp.arange(tp, dtype=jnp.int32)
    pos_s = past + jnp.arange(ts, dtype=jnp.int32)
    sc0 = jnp.zeros((bp,) + state_sconv.shape[2:], F32)
    cf0 = jnp.zeros((bp,) + state_cconv.shape[2:], F32)
    tiles_p = _tiles(bp * tp, tp, ffn)
    tiles_s = _tiles(bs * ts, ts, ffn)

    xp = x_prompt.reshape(bp * tp, d)
    xs = x_sample.reshape(bs * ts, d)
    rows_p, rows_s, sc_p, sc_s, cf_p, cf_s = [], [], [], [], [], []
    for l in range(depth):
        lam_init = 0.8 - 0.6 * math.exp(-0.3 * l)
        w = dict(norm_mix=norm_mix[l], w_in=_prep_w_in(w_in[l], bw), conv_sc=conv_sc[l], qk_norm=qk_norm[l],
                 conv_conf=conv_conf[l], conv_conf_bias=conv_conf_bias[l], conf_ln=conf_ln[l],
                 w_gate=w_gate[l].astype(BF16), w_branch=w_branch[l].astype(BF16),
                 w_out=w_out[l].astype(BF16), norm_ffn=norm_ffn[l], w_ffn_gate=w_ffn_gate[l].astype(BF16),
                 w_ffn_up=w_ffn_up[l].astype(BF16), w_ffn_down=w_ffn_down[l].astype(BF16))
        dl, sub = diff_lambda[l], diff_subln[l]

        def diff_p(dq, dk, dv):
            return _diff_attn_prompt(dq, dk, dv, dl, sub, bp, tp, lam_init, _pick_tile(tp, (256, 128)))

        def dsa_p(sq, sk, sv, iq, iklo, ikhi, ikw):
            return _dsa_attn_prompt(sq, sk, sv, iq, iklo, ikhi, ikw, bp, tp, _pick_tile(tp, (128,)))

        xp, scn, cfn, rows = _layer(xp, bp, tp, pos_p, sc0, cf0, diff_p, dsa_p, w, tiles_p)
        rows_p.append(rows)
        sc_p.append(scn)
        cf_p.append(cfn)

        def diff_s(dq, dk, dv):
            return _diff_attn_sample(dq, dk, dv, cache_diff_k, cache_diff_v, l, page_table, dl, sub,
                                     bs, ts, lam_init)

        def dsa_s(sq, sk, sv, iq, iklo, ikhi, ikw):
            return _dsa_attn_sample(sq, sk, sv, iq, iklo, ikw, cache_dsa_k, cache_dsa_v, cache_dsa_kidx,
                                    l, page_table, bs, ts)

        xs, scn, cfn, rows = _layer(xs, bs, ts, pos_s, state_sconv[l], state_cconv[l], diff_s, dsa_s, w,
                                    tiles_s)
        rows_s.append(rows)
        sc_s.append(scn)
        cf_s.append(cfn)

    def stack(rows, i):
        return jnp.stack([r[i] for r in rows])

    return (xp.reshape(bp, tp, d), xs.reshape(bs, ts, d),
            stack(rows_p, 0), stack(rows_s, 0), stack(rows_p, 1), stack(rows_s, 1),
            stack(rows_p, 2), stack(rows_s, 2), stack(rows_p, 3), stack(rows_s, 3),
            stack(rows_p, 4), stack(rows_s, 4),
            jnp.stack(sc_p), jnp.stack(sc_s), jnp.stack(cf_p), jnp.stack(cf_s))
```

```python
import functools
import math

import jax
import jax.numpy as jnp
from jax import lax
from jax.experimental import pallas as pl
from jax.experimental.pallas import tpu as pltpu

F32 = jnp.float32
BF16 = jnp.bfloat16

HEAD_DIM = 128
IDX_HEADS = 8
IDX_DIM = 64
TOPK_MAX = 256
PAGE_SIZE = 128
ROPE_THETA = 10000.0
NORM_EPS = 1e-6
LANES = 128
SUBLANES = 8
VMEM_LIMIT = 56 * 1024 * 1024
NEG_BIG = -1e30

(U_SC_IN, U_SC_B, U_SC_C, U_DQ, U_DK, U_DV, U_SQ, U_SK, U_SV, U_IDX, U_CF_A, U_CF_B) = range(12)
N_UBLOCKS = 12


def _cparams(n_axes):
    return pltpu.CompilerParams(dimension_semantics=("arbitrary",) * n_axes,
                                vmem_limit_bytes=VMEM_LIMIT)


def _dot(a, b):
    return jnp.dot(a, b, preferred_element_type=F32)


def _dot_nt(a, b):
    return lax.dot_general(a, b, (((1,), (1,)), ((), ())), preferred_element_type=F32)


def _sigmoid(x):
    return 1.0 / (1.0 + jnp.exp(-x))


def _pick_tile(n, candidates):
    for c in candidates:
        if n % c == 0:
            return c
    return n


def _rmsnorm_kernel(x_ref, g_ref, o_ref):
    x = x_ref[...]
    y = x * lax.rsqrt(jnp.mean(x * x, axis=-1, keepdims=True) + NORM_EPS)
    o_ref[...] = (y * g_ref[...]).astype(o_ref.dtype)


def _rmsnorm(x, g):
    m, d = x.shape
    tm = _pick_tile(m, (256, 128, 64, 32, 16, 8))
    return pl.pallas_call(
        _rmsnorm_kernel,
        grid=(m // tm,),
        in_specs=[pl.BlockSpec((tm, d), lambda i: (i, 0)),
                  pl.BlockSpec((1, d), lambda i: (0, 0))],
        out_specs=pl.BlockSpec((tm, d), lambda i: (i, 0)),
        out_shape=jax.ShapeDtypeStruct((m, d), BF16),
        compiler_params=_cparams(1),
        name="rmsnorm",
    )(x, g.reshape(1, d))


def _mm_kernel(x_ref, w_ref, o_ref):
    o_ref[...] = _dot(x_ref[...], w_ref[...]).astype(o_ref.dtype)


def _matmul(x, w, tm, tn, out_dtype):
    m, k = x.shape
    n = w.shape[1]
    return pl.pallas_call(
        _mm_kernel,
        grid=(m // tm, n // tn),
        in_specs=[pl.BlockSpec((tm, k), lambda i, j: (i, 0)),
                  pl.BlockSpec((k, tn), lambda i, j: (0, j))],
        out_specs=pl.BlockSpec((tm, tn), lambda i, j: (i, j)),
        out_shape=jax.ShapeDtypeStruct((m, n), out_dtype),
        compiler_params=_cparams(2),
        name="in_proj",
    )(x, w)


def _merge_kernel(h_ref, y_ref, wg_ref, wb_ref, o_ref, acc_ref):
    n = pl.program_id(2)
    gate = _sigmoid(_dot(h_ref[...], wg_ref[...]))
    val = gate * _dot(y_ref[...], wb_ref[...])

    @pl.when(n == 0)
    def _():
        acc_ref[...] = val

    @pl.when(n > 0)
    def _():
        acc_ref[...] += val

    @pl.when(n == pl.num_programs(2) - 1)
    def _():
        o_ref[...] = acc_ref[...].astype(o_ref.dtype)


def _merge(h, y, wg, wb, tm, tn):
    m, d = h.shape
    nb, _, bw = y.shape
    return pl.pallas_call(
        _merge_kernel,
        grid=(m // tm, d // tn, nb),
        in_specs=[pl.BlockSpec((tm, d), lambda i, j, n: (i, 0)),
                  pl.BlockSpec((None, tm, bw), lambda i, j, n: (n, i, 0)),
                  pl.BlockSpec((None, d, tn), lambda i, j, n: (n, 0, j)),
                  pl.BlockSpec((None, bw, tn), lambda i, j, n: (n, 0, j))],
        out_specs=pl.BlockSpec((tm, tn), lambda i, j, n: (i, j)),
        out_shape=jax.ShapeDtypeStruct((m, d), BF16),
        scratch_shapes=[pltpu.VMEM((tm, tn), F32)],
        compiler_params=_cparams(3),
        name="gated_merge",
    )(h, y, wg, wb)


def _resid_mm_kernel(x_ref, a_ref, w_ref, o_ref):
    o_ref[...] = x_ref[...] + _dot(a_ref[...], w_ref[...])


def _resid_matmul(x, a, w, tm, tn, name):
    m, k = a.shape
    n = w.shape[1]
    return pl.pallas_call(
        _resid_mm_kernel,
        grid=(m // tm, n // tn),
        in_specs=[pl.BlockSpec((tm, tn), lambda i, j: (i, j)),
                  pl.BlockSpec((tm, k), lambda i, j: (i, 0)),
                  pl.BlockSpec((k, tn), lambda i, j: (0, j))],
        out_specs=pl.BlockSpec((tm, tn), lambda i, j: (i, j)),
        out_shape=jax.ShapeDtypeStruct((m, n), F32),
        compiler_params=_cparams(2),
        name=name,
    )(x, a, w)


def _ffn_up_kernel(h_ref, wg_ref, wu_ref, o_ref):
    h = h_ref[...]
    g = _dot(h, wg_ref[...].astype(BF16))
    u = _dot(h, wu_ref[...].astype(BF16))
    o_ref[...] = (g * _sigmoid(g) * u).astype(o_ref.dtype)


def _ffn_up(h, wg, wu, layer, tm, tn):
    m, k = h.shape
    n = wg.shape[-1]
    return pl.pallas_call(
        _ffn_up_kernel,
        grid=(m // tm, n // tn),
        in_specs=[pl.BlockSpec((tm, k), lambda i, j: (i, 0), pipeline_mode=pl.Buffered(1)),
                  pl.BlockSpec((None, k, tn), lambda i, j: (layer, 0, j)),
                  pl.BlockSpec((None, k, tn), lambda i, j: (layer, 0, j))],
        out_specs=pl.BlockSpec((tm, tn), lambda i, j: (i, j)),
        out_shape=jax.ShapeDtypeStruct((m, n), BF16),
        compiler_params=_cparams(2),
        name="ffn_up",
    )(h, wg, wu)


def _sconv_kernel(cin_ref, cb_ref, cc_ref, st_ref, w_ref, y_ref, ns_ref, ext_ref, *, tt, width):
    t = pl.program_id(1)
    hist = width - 1
    base = SUBLANES

    @pl.when(t == 0)
    def _():
        ext_ref[base - hist:base, :] = st_ref[...]

    @pl.when(t > 0)
    def _():
        ext_ref[base - hist:base, :] = ext_ref[base + tt - hist:base + tt, :]

    ext_ref[base:base + tt, :] = cc_ref[...] * cin_ref[...]
    z = None
    for k in range(width):
        term = w_ref[k:k + 1, :] * ext_ref[base - hist + k:base - hist + k + tt, :]
        z = term if z is None else z + term
    y_ref[...] = (cb_ref[...] * z).astype(y_ref.dtype)

    @pl.when(t == pl.num_programs(1) - 1)
    def _():
        ns_ref[...] = ext_ref[base + tt - hist:base + tt, :]


def _sconv(u3, state, w, tt):
    b, t, _ = u3.shape
    width, bw = w.shape
    assert width - 1 <= SUBLANES and (tt >= width - 1 or t == tt)

    def ublock(c):
        return pl.BlockSpec((None, tt, bw), lambda bi, ti, c=c: (bi, ti, c))

    return pl.pallas_call(
        functools.partial(_sconv_kernel, tt=tt, width=width),
        grid=(b, t // tt),
        in_specs=[ublock(U_SC_IN), ublock(U_SC_B), ublock(U_SC_C),
                  pl.BlockSpec((None, width - 1, bw), lambda bi, ti: (bi, 0, 0)),
                  pl.BlockSpec((width, bw), lambda bi, ti: (0, 0))],
        out_specs=[pl.BlockSpec((None, tt, bw), lambda bi, ti: (bi, ti, 0)),
                   pl.BlockSpec((None, width - 1, bw), lambda bi, ti: (bi, 0, 0))],
        out_shape=[jax.ShapeDtypeStruct((b, t, bw), BF16),
                   jax.ShapeDtypeStruct((b, width - 1, bw), F32)],
        scratch_shapes=[pltpu.VMEM((SUBLANES + tt, bw), F32)],
        compiler_params=_cparams(2),
        name="short_conv",
    )(u3, u3, u3, state, w)


def _cconv_kernel(ca_ref, cb_ref, st_ref, w_ref, bias_ref, ln_ref, y_ref, ns_ref, ext_ref, c_ref,
                  *, tt, width, base):
    t = pl.program_id(1)
    hist = width - 1
    bw = ca_ref.shape[-1]

    @pl.when(t == 0)
    def _():
        ext_ref[base - hist:base, :] = st_ref[...]

    @pl.when(t > 0)
    def _():
        ext_ref[base - hist:base, :] = ext_ref[base + tt - hist:base + tt, :]

    ext_ref[base:base + tt, :] = ca_ref[...] * _sigmoid(cb_ref[...])
    for j in range(bw // LANES):
        ls = slice(j * LANES, (j + 1) * LANES)
        acc = None
        for k in range(width):
            term = w_ref[k:k + 1, ls] * ext_ref[base - hist + k:base - hist + k + tt, ls]
            acc = term if acc is None else acc + term
        c_ref[:, ls] = acc + bias_ref[:, ls]
    c = c_ref[...]
    xc = c - jnp.mean(c, axis=-1, keepdims=True)
    yn = xc * lax.rsqrt(jnp.mean(xc * xc, axis=-1, keepdims=True) + NORM_EPS)
    yn = yn * ln_ref[0:1, :] + ln_ref[1:2, :]
    y_ref[...] = (yn * _sigmoid(yn)).astype(y_ref.dtype)

    @pl.when(t == pl.num_programs(1) - 1)
    def _():
        ns_ref[...] = ext_ref[base + tt - hist:base + tt, :]


def _cconv(u3, state, w, bias, ln, tt):
    b, t, _ = u3.shape
    width, bw = w.shape
    hist = width - 1
    base = -(-hist // SUBLANES) * SUBLANES
    assert tt >= hist or t == tt

    def ublock(c):
        return pl.BlockSpec((None, tt, bw), lambda bi, ti, c=c: (bi, ti, c))

    return pl.pallas_call(
        functools.partial(_cconv_kernel, tt=tt, width=width, base=base),
        grid=(b, t // tt),
        in_specs=[ublock(U_CF_A), ublock(U_CF_B),
                  pl.BlockSpec((None, hist, bw), lambda bi, ti: (bi, 0, 0)),
                  pl.BlockSpec((width, bw), lambda bi, ti: (0, 0)),
                  pl.BlockSpec((1, bw), lambda bi, ti: (0, 0)),
                  pl.BlockSpec((2, bw), lambda bi, ti: (0, 0))],
        out_specs=[pl.BlockSpec((None, tt, bw), lambda bi, ti: (bi, ti, 0)),
                   pl.BlockSpec((None, hist, bw), lambda bi, ti: (bi, 0, 0))],
        out_shape=[jax.ShapeDtypeStruct((b, t, bw), BF16),
                   jax.ShapeDtypeStruct((b, hist, bw), F32)],
        scratch_shapes=[pltpu.VMEM((base + tt, bw), F32), pltpu.VMEM((tt, bw), F32)],
        compiler_params=_cparams(2),
        name="conformer_conv",
    )(u3, u3, state, w, bias.reshape(1, bw), ln)


def _qkprep_kernel(dq_ref, dk_ref, dv_ref, sq_ref, sk_ref, sv_ref, idx_ref, g_ref,
                   cos_ref, sin_ref, cos64_ref, sina_ref, sinb_ref,
                   dqb_ref, dkf_ref, dkb_ref, dvb_ref, sqb_ref, skf_ref, skb_ref, svb_ref,
                   iqb_ref, ikw_ref, iklo_ref, ikhi_ref):
    cos = cos_ref[...]
    sin = sin_ref[...]
    n_heads = dq_ref.shape[-1] // HEAD_DIM

    def norm_rope(x, g):
        y = x * lax.rsqrt(jnp.mean(x * x, axis=-1, keepdims=True) + NORM_EPS) * g
        return y * cos + pltpu.roll(y, HEAD_DIM // 2, 1) * sin

    for h in range(n_heads):
        ls = slice(h * HEAD_DIM, (h + 1) * HEAD_DIM)
        dqb_ref[:, ls] = norm_rope(dq_ref[:, ls], g_ref[0:1, :]).astype(BF16)
        dk = norm_rope(dk_ref[:, ls], g_ref[1:2, :])
        dkf_ref[:, ls] = dk
        dkb_ref[:, ls] = dk.astype(BF16)
        sqb_ref[:, ls] = norm_rope(sq_ref[:, ls], g_ref[2:3, :]).astype(BF16)
        sk = norm_rope(sk_ref[:, ls], g_ref[3:4, :])
        skf_ref[:, ls] = sk
        skb_ref[:, ls] = sk.astype(BF16)
    dvb_ref[...] = dv_ref[...].astype(BF16)
    svb_ref[...] = sv_ref[...].astype(BF16)

    cos64 = cos64_ref[...]
    sina = sina_ref[...]
    sinb = sinb_ref[...]

    def rope64(x):
        return (x * cos64 + pltpu.roll(x, LANES - IDX_DIM // 2, 1) * sina
                + pltpu.roll(x, IDX_DIM // 2, 1) * sinb)

    n_iq = IDX_HEADS * IDX_DIM
    for j in range(n_iq // LANES):
        ls = slice(j * LANES, (j + 1) * LANES)
        iqb_ref[:, ls] = rope64(idx_ref[:, ls]).astype(BF16)
    xs = idx_ref[:, n_iq:n_iq + LANES]
    lane = lax.broadcasted_iota(jnp.int32, xs.shape, 1)
    ik = jnp.where(lane < IDX_DIM, rope64(xs), 0.0)
    iw = jnp.where((lane >= IDX_DIM) & (lane < IDX_DIM + IDX_HEADS), xs * (IDX_HEADS ** -0.5), 0.0)
    ikw_ref[...] = ik + iw
    iklo_ref[...] = ik.astype(BF16)
    ikhi_ref[...] = pltpu.roll(ik, IDX_DIM, 1).astype(BF16)


def _qkprep(u, qk_norm, tabs, tt, n_tblocks):
    m = u.shape[0]
    bw = u.shape[1] // N_UBLOCKS

    def ublock(c):
        return pl.BlockSpec((tt, bw), lambda i, c=c: (i, c))

    def tab():
        return pl.BlockSpec((tt, LANES), lambda i: (i % n_tblocks, 0))

    def out(w):
        return pl.BlockSpec((tt, w), lambda i: (i, 0))

    n_iq = IDX_HEADS * IDX_DIM
    shapes = [(bw, BF16), (bw, F32), (bw, BF16), (bw, BF16), (bw, BF16), (bw, F32), (bw, BF16), (bw, BF16),
              (n_iq, BF16), (LANES, F32), (LANES, BF16), (LANES, BF16)]
    return pl.pallas_call(
        _qkprep_kernel,
        grid=(m // tt,),
        in_specs=[ublock(U_DQ), ublock(U_DK), ublock(U_DV), ublock(U_SQ), ublock(U_SK), ublock(U_SV),
                  ublock(U_IDX), pl.BlockSpec((4, HEAD_DIM), lambda i: (0, 0)),
                  tab(), tab(), tab(), tab(), tab()],
        out_specs=[out(w) for w, _ in shapes],
        out_shape=[jax.ShapeDtypeStruct((m, w), dt) for w, dt in shapes],
        compiler_params=_cparams(1),
        name="qk_norm_rope",
    )(u, u, u, u, u, u, u, qk_norm, *tabs)


def _rope_tables(pos):
    def angles(d):
        inv = ROPE_THETA ** (-jnp.arange(0, d, 2, dtype=F32) / d)
        return pos.astype(F32)[:, None] * inv[None, :]

    a = angles(HEAD_DIM)
    cos = jnp.concatenate([jnp.cos(a), jnp.cos(a)], axis=1)
    sin = jnp.concatenate([-jnp.sin(a), jnp.sin(a)], axis=1)
    a2 = angles(IDX_DIM)
    c2, s2, z2 = jnp.cos(a2), jnp.sin(a2), jnp.zeros_like(a2)
    cos64 = jnp.concatenate([c2, c2, c2, c2], axis=1)
    sina = jnp.concatenate([-s2, z2, -s2, z2], axis=1)
    sinb = jnp.concatenate([z2, s2, z2, s2], axis=1)
    return (cos, sin, cos64, sina, sinb)


def _diff_lambda(dl_ref, lam_init):
    lv = dl_ref[...]
    a = jnp.sum(lv[0:1, :] * lv[1:2, :], axis=-1, keepdims=True)
    b = jnp.sum(lv[2:3, :] * lv[3:4, :], axis=-1, keepdims=True)
    return jnp.exp(a) - jnp.exp(b) + lam_init


def _subln(o, g, lam_init):
    y = o * lax.rsqrt(jnp.mean(o * o, axis=-1, keepdims=True) + NORM_EPS)
    return (y * g) * (1.0 - lam_init)


def _topk_select(score, col, k, idx_bits):
    r = score.shape[0]
    bits = pltpu.bitcast(score + 0.0, jnp.int32)
    key = jnp.where(bits < 0, bits ^ jnp.int32(0x7FFFFFFF), bits)
    kf = float(k)

    def count(pred):
        return jnp.sum(jnp.where(pred, 1.0, 0.0), axis=-1, keepdims=True)

    zero = jnp.zeros((r, 1), jnp.int32)
    start = jnp.where(count(key >= zero) >= kf, zero, jnp.full((r, 1), -2 ** 31, jnp.int32))

    def value_step(i, cur):
        cand = cur + jnp.left_shift(jnp.int32(1), 30 - i)
        return jnp.where(count(key >= cand) >= kf, cand, cur)

    thr = lax.fori_loop(0, 31, value_step, start)
    above = key > thr
    tie = key == thr
    need = kf - count(above)

    def index_step(i, cur):
        cand = cur + jnp.left_shift(jnp.int32(1), idx_bits - 1 - i)
        cnt = jnp.sum(jnp.where(tie, jnp.where(col < cand, 1.0, 0.0), 0.0), axis=-1, keepdims=True)
        return jnp.where(cnt <= need, cand, cur)

    bound = lax.fori_loop(0, idx_bits, index_step, zero)
    return above | (tie & (col < bound))


def _diff_attn_kernel(q_ref, k_ref, v_ref, dl_ref, g_ref, o_ref, *, tq, q0, lam_init):
    qi = pl.program_id(2)
    q = q_ref[...]
    k = k_ref[...]
    t = k.shape[0]
    row = q0 + qi * tq + lax.broadcasted_iota(jnp.int32, (tq, t), 0)
    col = lax.broadcasted_iota(jnp.int32, (tq, t), 1)
    mask = col <= row
    scale = HEAD_DIM ** -0.5

    def softmax(c):
        ls = slice(c * HEAD_DIM, (c + 1) * HEAD_DIM)
        s = jnp.where(mask, _dot_nt(q[:, ls], k[:, ls]) * scale, -jnp.inf)
        e = jnp.exp(s - jnp.max(s, axis=-1, keepdims=True))
        return e * (1.0 / jnp.sum(e, axis=-1, keepdims=True))

    lam = _diff_lambda(dl_ref, lam_init)
    p = softmax(0) - lam * softmax(1)
    o = _dot(p.astype(BF16), v_ref[...])
    o_ref[...] = _subln(o, g_ref[...], lam_init).astype(o_ref.dtype)


def _causal_groups(t, tq):
    n = _pick_tile(t // tq, (4, 2, 1))
    tg = t // n
    return [(g * tg, tg) for g in range(n)]


def _diff_attn_prompt(dq, dk, dv, diff_lambda, subln, b, t, lam_init, tq):
    bw = dq.shape[-1]
    vd = 2 * HEAD_DIM
    nh = bw // vd
    q3, k3, v3 = (a.reshape(b, t, bw) for a in (dq, dk, dv))
    outs = []
    for q0, tg in _causal_groups(t, tq):
        tk = q0 + tg
        qb0 = q0 // tq
        outs.append(pl.pallas_call(
            functools.partial(_diff_attn_kernel, tq=tq, q0=q0, lam_init=lam_init),
            grid=(b, nh, tg // tq),
            in_specs=[pl.BlockSpec((None, tq, vd), lambda bi, h, qi, qb0=qb0: (bi, qb0 + qi, h)),
                      pl.BlockSpec((None, tk, vd), lambda bi, h, qi: (bi, 0, h)),
                      pl.BlockSpec((None, tk, vd), lambda bi, h, qi: (bi, 0, h)),
                      pl.BlockSpec((4, HEAD_DIM), lambda bi, h, qi: (0, 0)),
                      pl.BlockSpec((1, vd), lambda bi, h, qi: (0, 0))],
            out_specs=pl.BlockSpec((None, tq, vd), lambda bi, h, qi: (bi, qi, h)),
            out_shape=jax.ShapeDtypeStruct((b, tg, bw), BF16),
            compiler_params=_cparams(3),
            name="diff_attn_prompt",
        )(q3, k3, v3, diff_lambda, subln.reshape(1, vd)))
    return jnp.concatenate(outs, axis=1)


def _dsa_attn_kernel(q_ref, k_ref, v_ref, iq_ref, iklo_ref, ikhi_ref, iw_ref, o_ref, *, tq, q0, topk,
                     idx_bits):
    qi = pl.program_id(1)
    t = k_ref.shape[0]
    iklo = iklo_ref[...]
    ikhi = ikhi_ref[...]
    iw = iw_ref[...]
    score = None
    for j in range(IDX_HEADS // 2):
        qpair = iq_ref[:, j * LANES:(j + 1) * LANES]
        for half, ik in enumerate((iklo, ikhi)):
            h = 2 * j + half
            sc = jnp.maximum(_dot_nt(qpair, ik) * (IDX_DIM ** -0.5), 0.0)
            term = iw[:, IDX_DIM + h:IDX_DIM + h + 1] * sc
            score = term if score is None else score + term
    row = q0 + qi * tq + lax.broadcasted_iota(jnp.int32, (tq, t), 0)
    col = lax.broadcasted_iota(jnp.int32, (tq, t), 1)
    causal = col <= row
    sel = _topk_select(jnp.where(causal, score, -jnp.inf), col, topk, idx_bits)
    keep = causal & sel
    scale = HEAD_DIM ** -0.5
    for h in range(q_ref.shape[-1] // HEAD_DIM):
        ls = slice(h * HEAD_DIM, (h + 1) * HEAD_DIM)
        s = jnp.where(keep, _dot_nt(q_ref[:, ls], k_ref[:, ls]) * scale, -jnp.inf)
        e = jnp.exp(s - jnp.max(s, axis=-1, keepdims=True))
        a = e * (1.0 / jnp.sum(e, axis=-1, keepdims=True))
        o_ref[:, ls] = _dot(a.astype(BF16), v_ref[:, ls]).astype(o_ref.dtype)


def _dsa_attn_prompt(sq, sk, sv, iq, iklo, ikhi, ikw, b, t, tq):
    bw = sq.shape[-1]
    n_iq = iq.shape[-1]
    topk = min(TOPK_MAX, t // 4)
    args = (sq.reshape(b, t, bw), sk.reshape(b, t, bw), sv.reshape(b, t, bw), iq.reshape(b, t, n_iq),
            iklo.reshape(b, t, LANES), ikhi.reshape(b, t, LANES), ikw.reshape(b, t, LANES))
    outs = []
    for q0, tg in _causal_groups(t, tq):
        tk = q0 + tg
        qb0 = q0 // tq

        def qblock(w, qb0=qb0):
            return pl.BlockSpec((None, tq, w), lambda bi, qi: (bi, qb0 + qi, 0))

        def kblock(w, tk=tk):
            return pl.BlockSpec((None, tk, w), lambda bi, qi: (bi, 0, 0))

        outs.append(pl.pallas_call(
            functools.partial(_dsa_attn_kernel, tq=tq, q0=q0, topk=topk, idx_bits=int(tk).bit_length()),
            grid=(b, tg // tq),
            in_specs=[qblock(bw), kblock(bw), kblock(bw), qblock(n_iq), kblock(LANES), kblock(LANES),
                      qblock(LANES)],
            out_specs=pl.BlockSpec((None, tq, bw), lambda bi, qi: (bi, qi, 0)),
            out_shape=jax.ShapeDtypeStruct((b, tg, bw), BF16),
            compiler_params=_cparams(2),
            name="dsa_attn_prompt",
        )(*args))
    return jnp.concatenate(outs, axis=1)


GROUPS = SUBLANES
PAGE_ROWS = PAGE_SIZE * GROUPS


def _pages_per_step(n_pages):
    return _pick_tile(n_pages, (8, 4, 2, 1))


def _page_specs(layer, npp, n_steps, rows, width):
    def spec(r):
        def index(bi, p, pt):
            return (layer, pt[bi, jnp.minimum(p, n_steps - 1) * npp + r], 0, 0)
        return pl.BlockSpec((None, None, rows, width), index)
    return [spec(r) for r in range(npp)]


def _group_match(shape):
    lane = lax.broadcasted_iota(jnp.int32, shape, 1)
    row = lax.broadcasted_iota(jnp.int32, shape, 0)
    return (lane & (GROUPS - 1)) == (row >> 3)


def _softmax_update(s, keep, m_ref, l_ref):
    s = jnp.where(keep, s, NEG_BIG)
    m_old = m_ref[...]
    m_new = jnp.maximum(m_old, jnp.max(s, axis=-1, keepdims=True))
    alpha = jnp.exp(m_old - m_new)
    p = jnp.where(keep, jnp.exp(s - m_new), 0.0)
    l_ref[...] = alpha * l_ref[...] + jnp.sum(p, axis=-1, keepdims=True)
    m_ref[...] = m_new
    return p, alpha


def _init_softmax_state(m_ref, l_ref, acc_ref):
    m_ref[...] = jnp.full(m_ref.shape, NEG_BIG, F32)
    l_ref[...] = jnp.zeros(l_ref.shape, F32)
    acc_ref[...] = jnp.zeros(acc_ref.shape, F32)


def _new_key_mask(shape, t_new):
    key = lax.broadcasted_iota(jnp.int32, shape, 1) >> 3
    qrow = lax.broadcasted_iota(jnp.int32, shape, 0) & (SUBLANES - 1)
    return _group_match(shape) & (key <= qrow) & (key < t_new)


def _value_row_shift(h, c, half, n_heads):
    return (half * n_heads + h) - (2 * h + c)


def _spread_weights(p, n_heads):
    n = p.shape[1]
    tiles = []
    for h in range(n_heads):
        for c in range(2):
            tile = p[(2 * h + c) * SUBLANES:(2 * h + c + 1) * SUBLANES, :]
            for half in range(2):
                shift = _value_row_shift(h, c, half, n_heads) % n
                tiles.append(tile if shift == 0 else pltpu.roll(tile, shift, 1))
    return jnp.concatenate(tiles, axis=0)


def _spread_rows(a):
    tiles = []
    for g in range(GROUPS):
        tile = a[g * SUBLANES:(g + 1) * SUBLANES, :]
        tiles += [tile, tile]
    return jnp.concatenate(tiles, axis=0)


def _diff_attn_sample_kernel(pt_ref, q_ref, *refs, npp, n_steps, t_new, lam_init):
    k_refs = refs[:npp]
    v_refs = refs[npp:2 * npp]
    kn_ref, vn_ref, dl_ref, g_ref, o_ref, m_ref, l_ref, acc_ref = refs[2 * npp:]
    p = pl.program_id(1)
    n_heads = GROUPS // 2
    scale = HEAD_DIM ** -0.5

    @pl.when(p == 0)
    def _():
        _init_softmax_state(m_ref, l_ref, acc_ref)

    def accumulate(w, alpha, v_parts):
        pv = None
        for r, v in enumerate(v_parts):
            rows = v.shape[0]
            part = _dot(_spread_weights(w[:, r * rows:(r + 1) * rows], n_heads).astype(BF16), v)
            pv = part if pv is None else pv + part
        acc_ref[...] = _spread_rows(alpha) * acc_ref[...] + pv

    @pl.when(p < n_steps)
    def _():
        q = q_ref[...]
        s = jnp.concatenate([_dot_nt(q, kr[...].astype(BF16)) for kr in k_refs], axis=1) * scale
        w, alpha = _softmax_update(s, _group_match(s.shape), m_ref, l_ref)
        accumulate(w, alpha, [vr[...].astype(BF16) for vr in v_refs])

    @pl.when(p == n_steps)
    def _():
        s = _dot_nt(q_ref[...], kn_ref[...]) * scale
        w, alpha = _softmax_update(s, _new_key_mask(s.shape, t_new), m_ref, l_ref)
        accumulate(w, alpha, [vn_ref[...]])
        out = acc_ref[...] * _spread_rows(1.0 / l_ref[...])
        lam = _diff_lambda(dl_ref, lam_init)
        vd = 2 * HEAD_DIM
        for h in range(n_heads):
            def tile(c, half):
                r0 = ((2 * h + c) * 2 + half) * SUBLANES
                return out[r0:r0 + SUBLANES, :]
            o = jnp.concatenate([tile(0, half) - lam * tile(1, half) for half in range(2)], axis=1)
            o_ref[:, h * vd:(h + 1) * vd] = _subln(o, g_ref[...], lam_init).astype(o_ref.dtype)


def _group_major_queries(q, b, t):
    qg = q.reshape(b, t, GROUPS, HEAD_DIM).transpose(0, 2, 1, 3)
    qg = jnp.pad(qg, ((0, 0), (0, 0), (0, SUBLANES - t), (0, 0)))
    return qg.reshape(b, GROUPS * SUBLANES, HEAD_DIM)


def _pad_rows(a, rows):
    return jnp.pad(a, ((0, 0), (0, rows - a.shape[1]), (0, 0)))


def _value_rows(v, lead):
    n_heads = GROUPS // 2
    v = v.reshape(lead + (-1, n_heads, 2, HEAD_DIM))
    v = jnp.swapaxes(v, -3, -2)
    return v.reshape(lead + (-1, HEAD_DIM))


def _diff_attn_sample(dq, dk, dv, cache_k, cache_v, layer, page_table, diff_lambda, subln, b, t, lam_init):
    bw = dq.shape[-1]
    vd = 2 * HEAD_DIM
    n_pages = page_table.shape[1]
    npp = _pages_per_step(n_pages)
    n_steps = n_pages // npp
    rows = GROUPS * SUBLANES
    new_rows = LANES
    qg = _group_major_queries(dq, b, t)
    kn = _pad_rows(dk.reshape(b, t * GROUPS, HEAD_DIM), new_rows)
    vn = _pad_rows(_value_rows(dv.reshape(b, t, GROUPS // 2, vd), (b,)), new_rows)

    def per_batch(r, w):
        return pl.BlockSpec((None, r, w), lambda bi, p, pt: (bi, 0, 0))

    grid_spec = pltpu.PrefetchScalarGridSpec(
        num_scalar_prefetch=1,
        grid=(b, n_steps + 1),
        in_specs=[per_batch(rows, HEAD_DIM)]
        + _page_specs(layer, npp, n_steps, PAGE_ROWS, HEAD_DIM)
        + _page_specs(layer, npp, n_steps, PAGE_ROWS, HEAD_DIM)
        + [per_batch(new_rows, HEAD_DIM), per_batch(new_rows, HEAD_DIM),
           pl.BlockSpec((4, HEAD_DIM), lambda bi, p, pt: (0, 0)),
           pl.BlockSpec((1, vd), lambda bi, p, pt: (0, 0))],
        out_specs=per_batch(SUBLANES, bw),
        scratch_shapes=[pltpu.VMEM((rows, 1), F32), pltpu.VMEM((rows, 1), F32),
                        pltpu.VMEM((2 * rows, HEAD_DIM), F32)],
    )
    out = pl.pallas_call(
        functools.partial(_diff_attn_sample_kernel, npp=npp, n_steps=n_steps, t_new=t, lam_init=lam_init),
        grid_spec=grid_spec,
        out_shape=jax.ShapeDtypeStruct((b, SUBLANES, bw), BF16),
        compiler_params=_cparams(2),
        name="diff_attn_sample",
    )(page_table, qg, *([cache_k] * npp), *([cache_v] * npp), kn, vn, diff_lambda, subln.reshape(1, vd))
    return out[:, :t].reshape(b * t, bw)


def _idx_score_kernel(pt_ref, q_ref, w_ref, *refs, npp, n_steps):
    k_refs = refs[:npp]
    kn_ref, o_ref = refs[npp:]
    p = pl.program_id(1)
    q = q_ref[...]
    w = w_ref[...]

    def scores(kt):
        sc = jnp.maximum(_dot(q, kt) * (IDX_DIM ** -0.5), 0.0) * w
        out = sc[0:SUBLANES]
        for h in range(1, IDX_HEADS):
            out = out + sc[h * SUBLANES:(h + 1) * SUBLANES]
        return out

    @pl.when(p < n_steps)
    def _():
        for r, kr in enumerate(k_refs):
            o_ref[:, r * PAGE_SIZE:(r + 1) * PAGE_SIZE] = scores(kr[...].astype(BF16))

    @pl.when(p == n_steps)
    def _():
        o_ref[...] = scores(kn_ref[...])


def _select_sample_kernel(s_ref, o_ref, *, past, t_new, topk, idx_bits):
    s = s_ref[...]
    qrow = jnp.minimum(lax.broadcasted_iota(jnp.int32, s.shape, 0), t_new - 1)
    col = lax.broadcasted_iota(jnp.int32, s.shape, 1)
    admissible = (col <= past + qrow) & (col < past + t_new)
    sel = _topk_select(jnp.where(admissible, s, -jnp.inf), col, topk, idx_bits)
    o_ref[...] = jnp.where(admissible, jnp.where(sel, 1.0, 0.0), 0.0)


def _dsa_attn_sample_kernel(pt_ref, q_ref, keep_ref, *refs, npp, n_steps):
    k_refs = refs[:npp]
    v_refs = refs[npp:2 * npp]
    kn_ref, vn_ref, o_ref, m_ref, l_ref, acc_ref = refs[2 * npp:]
    p = pl.program_id(1)
    scale = HEAD_DIM ** -0.5

    @pl.when(p == 0)
    def _():
        _init_softmax_state(m_ref, l_ref, acc_ref)

    def keep_mask(n):
        kp = keep_ref[:, 0:n] > 0.5
        return _group_match((GROUPS * SUBLANES, n)) & jnp.concatenate([kp] * GROUPS, axis=0)

    def accumulate(w, alpha, v_parts):
        pv = None
        for r, v in enumerate(v_parts):
            rows = v.shape[0]
            part = _dot(w[:, r * rows:(r + 1) * rows].astype(BF16), v)
            pv = part if pv is None else pv + part
        acc_ref[...] = alpha * acc_ref[...] + pv

    @pl.when(p < n_steps)
    def _():
        q = q_ref[...]
        s = jnp.concatenate([_dot_nt(q, kr[...].astype(BF16)) for kr in k_refs], axis=1) * scale
        w, alpha = _softmax_update(s, keep_mask(s.shape[1]), m_ref, l_ref)
        accumulate(w, alpha, [vr[...].astype(BF16) for vr in v_refs])

    @pl.when(p == n_steps)
    def _():
        s = _dot_nt(q_ref[...], kn_ref[...]) * scale
        w, alpha = _softmax_update(s, keep_mask(s.shape[1]), m_ref, l_ref)
        accumulate(w, alpha, [vn_ref[...]])
        o_ref[...] = (acc_ref[...] * (1.0 / l_ref[...])).astype(o_ref.dtype)


def _dsa_attn_sample(sq, sk, sv, iq, ikb, ikw, cache_k, cache_v, cache_kidx, layer, page_table, b, t):
    bw = sq.shape[-1]
    n_pages = page_table.shape[1]
    past = n_pages * PAGE_SIZE
    npp = _pages_per_step(n_pages)
    n_steps = n_pages // npp
    step_w = npp * PAGE_SIZE
    s_pad = (n_steps + 1) * step_w
    topk = min(TOPK_MAX, (past + t) // 4)
    idx_bits = int(s_pad).bit_length()

    def per_batch(r, w):
        return pl.BlockSpec((None, r, w), lambda bi, p, pt: (bi, 0, 0))

    n_iq = iq.shape[-1]
    qi = iq.reshape(b, t, IDX_HEADS, IDX_DIM).transpose(0, 2, 1, 3)
    qi = jnp.pad(qi, ((0, 0), (0, 0), (0, SUBLANES - t), (0, 0))).reshape(b, IDX_HEADS * SUBLANES, IDX_DIM)
    wi = ikw.reshape(b, t, LANES)[:, :, IDX_DIM:IDX_DIM + IDX_HEADS].transpose(0, 2, 1)
    wi = jnp.pad(wi, ((0, 0), (0, 0), (0, SUBLANES - t))).reshape(b, IDX_HEADS * SUBLANES, 1)
    kin = jnp.swapaxes(_pad_rows(ikb.reshape(b, t, LANES)[:, :, :IDX_DIM], step_w), 1, 2)
    scores = pl.pallas_call(
        functools.partial(_idx_score_kernel, npp=npp, n_steps=n_steps),
        grid_spec=pltpu.PrefetchScalarGridSpec(
            num_scalar_prefetch=1,
            grid=(b, n_steps + 1),
            in_specs=[per_batch(IDX_HEADS * SUBLANES, IDX_DIM), per_batch(IDX_HEADS * SUBLANES, 1)]
            + _page_specs(layer, npp, n_steps, IDX_DIM, PAGE_SIZE) + [per_batch(IDX_DIM, step_w)],
            out_specs=pl.BlockSpec((None, SUBLANES, step_w), lambda bi, p, pt: (bi, 0, p)),
        ),
        out_shape=jax.ShapeDtypeStruct((b, SUBLANES, s_pad), F32),
        compiler_params=_cparams(2),
        name="idx_score_sample",
    )(page_table, qi, wi, *([cache_kidx] * npp), kin)

    keep = pl.pallas_call(
        functools.partial(_select_sample_kernel, past=past, t_new=t, topk=topk, idx_bits=idx_bits),
        grid=(b,),
        in_specs=[pl.BlockSpec((None, SUBLANES, s_pad), lambda bi: (bi, 0, 0))],
        out_specs=pl.BlockSpec((None, SUBLANES, s_pad), lambda bi: (bi, 0, 0)),
        out_shape=jax.ShapeDtypeStruct((b, SUBLANES, s_pad), F32),
        compiler_params=_cparams(1),
        name="topk_select_sample",
    )(scores)

    rows = GROUPS * SUBLANES
    new_rows = LANES
    keep_lanes = jnp.repeat(keep, GROUPS, axis=2)
    qg = _group_major_queries(sq, b, t)
    kn = _pad_rows(sk.reshape(b, t * GROUPS, HEAD_DIM), new_rows)
    vn = _pad_rows(sv.reshape(b, t * GROUPS, HEAD_DIM), new_rows)
    out = pl.pallas_call(
        functools.partial(_dsa_attn_sample_kernel, npp=npp, n_steps=n_steps),
        grid_spec=pltpu.PrefetchScalarGridSpec(
            num_scalar_prefetch=1,
            grid=(b, n_steps + 1),
            in_specs=[per_batch(rows, HEAD_DIM),
                      pl.BlockSpec((None, SUBLANES, step_w * GROUPS), lambda bi, p, pt: (bi, 0, p))]
            + _page_specs(layer, npp, n_steps, PAGE_ROWS, HEAD_DIM)
            + _page_specs(layer, npp, n_steps, PAGE_ROWS, HEAD_DIM)
            + [per_batch(new_rows, HEAD_DIM), per_batch(new_rows, HEAD_DIM)],
            out_specs=per_batch(rows, HEAD_DIM),
            scratch_shapes=[pltpu.VMEM((rows, 1), F32), pltpu.VMEM((rows, 1), F32),
                            pltpu.VMEM((rows, HEAD_DIM), F32)],
        ),
        out_shape=jax.ShapeDtypeStruct((b, rows, HEAD_DIM), BF16),
        compiler_params=_cparams(2),
        name="dsa_attn_sample",
    )(page_table, qg, keep_lanes, *([cache_k] * npp), *([cache_v] * npp), kn, vn)
    out = out.reshape(b, GROUPS, SUBLANES, HEAD_DIM)[:, :, :t].transpose(0, 2, 1, 3)
    return out.reshape(b * t, bw)


def _prep_w_in(w, bw):
    n_idx = IDX_HEADS * IDX_DIM + IDX_DIM + IDX_HEADS
    wb = w.astype(BF16)
    main = wb[:, :9 * bw]
    idx = jnp.pad(wb[:, 9 * bw:9 * bw + n_idx], ((0, 0), (0, bw - n_idx)))
    return jnp.concatenate([main, idx, wb[:, 9 * bw + n_idx:]], axis=1)


def _layer(x, b, t, pos, sc_state, cf_state, attend_diff, attend_dsa, w, tiles):
    m, d = x.shape
    bw = d // 4
    h = _rmsnorm(x, w["norm_mix"])
    u = _matmul(h, w["w_in"], tiles["tm"], tiles["tn_in"], F32)
    u3 = u.reshape(b, t, N_UBLOCKS * bw)

    y_a, sc_new = _sconv(u3, sc_state, w["conv_sc"], tiles["tt_conv"])
    y_d, cf_new = _cconv(u3, cf_state, w["conv_conf"], w["conv_conf_bias"], w["conf_ln"], tiles["tt_conv"])

    tabs = _rope_tables(pos)
    if tiles["tt_prep"] > t:
        tabs = tuple(jnp.tile(a, (tiles["tt_prep"] // t, 1)) for a in tabs)
    n_tblocks = max(1, t // tiles["tt_prep"])
    (dqb, dkf, dkb, dvb, sqb, skf, skb, svb, iqb, ikw, iklo, ikhi) = _qkprep(
        u, w["qk_norm"], tabs, tiles["tt_prep"], n_tblocks)

    y_b = attend_diff(dqb, dkb, dvb)
    y_c = attend_dsa(sqb, skb, svb, iqb, iklo, ikhi, ikw)

    y = jnp.stack([y_a.reshape(m, bw), y_b.reshape(m, bw), y_c.reshape(m, bw), y_d.reshape(m, bw)])
    merged = _merge(h, y, w["w_gate"], w["w_branch"], tiles["tm"], tiles["tn_merge"])
    x = _resid_matmul(x, merged, w["w_out"], tiles["tm"], tiles["tn_out"], "out_proj")

    h2 = _rmsnorm(x, w["norm_ffn"])
    a = _ffn_up(h2, w["w_ffn_gate"], w["w_ffn_up"], w["layer"], tiles["tm_ffn"], tiles["tn_ffn"])
    x = _resid_matmul(x, a, w["w_ffn_down"], tiles["tm_down"], tiles["tn_down"], "ffn_down")

    dv = u3[:, :, U_DV * bw:(U_DV + 1) * bw]
    sv = u3[:, :, U_SV * bw:(U_SV + 1) * bw]
    rows = (dkf.reshape(b, t, bw // (2 * HEAD_DIM), 2, HEAD_DIM),
            dv.reshape(b, t, bw // (2 * HEAD_DIM), 2 * HEAD_DIM),
            skf.reshape(b, t, bw // HEAD_DIM, HEAD_DIM),
            sv.reshape(b, t, bw // HEAD_DIM, HEAD_DIM),
            ikw.reshape(b, t, LANES)[:, :, :IDX_DIM])
    return x, sc_new, cf_new, rows


def _tiles(m, t, ffn):
    if m >= 1024:
        tm = _pick_tile(m, (1024, 512, 256))
        return dict(tm=tm, tn_in=1024, tn_merge=512, tn_out=1024,
                    tm_ffn=_pick_tile(m, (2048, 1024, 512, 256)), tn_ffn=_pick_tile(ffn, (256, 128)),
                    tm_down=_pick_tile(m, (512, 256)), tn_down=256,
                    tt_conv=_pick_tile(t, (128,)), tt_prep=_pick_tile(t, (256, 128)))
    return dict(tm=m, tn_in=1024, tn_merge=512, tn_out=1024, tm_ffn=m, tn_ffn=_pick_tile(ffn, (256, 128)),
                tm_down=m, tn_down=512, tt_conv=t, tt_prep=m)


def kernel(x_prompt, x_sample, cache_diff_k, cache_diff_v, cache_dsa_k, cache_dsa_v, cache_dsa_kidx,
           state_sconv, state_cconv, page_table, norm_mix, w_in, conv_sc, qk_norm, diff_lambda, diff_subln,
           conv_conf, conv_conf_bias, conf_ln, w_gate, w_branch, w_out, norm_ffn, w_ffn_gate, w_ffn_up,
           w_ffn_down):
    bp, tp, d = x_prompt.shape
    bs, ts, _ = x_sample.shape
    depth = w_in.shape[0]
    bw = d // 4
    ffn = w_ffn_gate.shape[-1]
    past = page_table.shape[1] * PAGE_SIZE
    pos_p = jnp.arange(tp, dtype=jnp.int32)
    pos_s = past + jnp.arange(ts, dtype=jnp.int32)
    sc0 = jnp.zeros((bp,) + state_sconv.shape[2:], F32)
    cf0 = jnp.zeros((bp,) + state_cconv.shape[2:], F32)
    tiles_p = _tiles(bp * tp, tp, ffn)
    tiles_s = _tiles(bs * ts, ts, ffn)
    assert bw == GROUPS * HEAD_DIM and cache_diff_k.shape[2] == PAGE_SIZE

    lead = cache_diff_k.shape[:2]
    page_diff_k = cache_diff_k.reshape(lead + (PAGE_ROWS, HEAD_DIM))
    page_diff_v = _value_rows(cache_diff_v, lead)
    page_dsa_k = cache_dsa_k.reshape(lead + (PAGE_ROWS, HEAD_DIM))
    page_dsa_v = cache_dsa_v.reshape(lead + (PAGE_ROWS, HEAD_DIM))
    page_kidx_t = jnp.swapaxes(cache_dsa_kidx, 2, 3)

    xp = x_prompt.reshape(bp * tp, d)
    xs = x_sample.reshape(bs * ts, d)
    rows_p, rows_s, sc_p, sc_s, cf_p, cf_s = [], [], [], [], [], []
    for l in range(depth):
        lam_init = 0.8 - 0.6 * math.exp(-0.3 * l)
        w = dict(norm_mix=norm_mix[l], w_in=_prep_w_in(w_in[l], bw), conv_sc=conv_sc[l], qk_norm=qk_norm[l],
                 conv_conf=conv_conf[l], conv_conf_bias=conv_conf_bias[l], conf_ln=conf_ln[l],
                 w_gate=w_gate[l].astype(BF16), w_branch=w_branch[l].astype(BF16),
                 w_out=w_out[l].astype(BF16), norm_ffn=norm_ffn[l], w_ffn_gate=w_ffn_gate,
                 w_ffn_up=w_ffn_up, layer=l, w_ffn_down=w_ffn_down[l].astype(BF16))
        dl, sub = diff_lambda[l], diff_subln[l]

        def diff_p(dq, dk, dv):
            return _diff_attn_prompt(dq, dk, dv, dl, sub, bp, tp, lam_init, _pick_tile(tp, (256, 128)))

        def dsa_p(sq, sk, sv, iq, iklo, ikhi, ikw):
            return _dsa_attn_prompt(sq, sk, sv, iq, iklo, ikhi, ikw, bp, tp, _pick_tile(tp, (128,)))

        xp, scn, cfn, rows = _layer(xp, bp, tp, pos_p, sc0, cf0, diff_p, dsa_p, w, tiles_p)
        rows_p.append(rows)
        sc_p.append(scn)
        cf_p.append(cfn)

        def diff_s(dq, dk, dv):
            return _diff_attn_sample(dq, dk, dv, page_diff_k, page_diff_v, l, page_table, dl, sub,
                                     bs, ts, lam_init)

        def dsa_s(sq, sk, sv, iq, iklo, ikhi, ikw):
            return _dsa_attn_sample(sq, sk, sv, iq, iklo, ikw, page_dsa_k, page_dsa_v, page_kidx_t,
                                    l, page_table, bs, ts)

        xs, scn, cfn, rows = _layer(xs, bs, ts, pos_s, state_sconv[l], state_cconv[l], diff_s, dsa_s, w,
                                    tiles_s)
        rows_s.append(rows)
        sc_s.append(scn)
        cf_s.append(cfn)

    def stack(rows, i):
        return jnp.stack([r[i] for r in rows])

    return (xp.reshape(bp, tp, d), xs.reshape(bs, ts, d),
            stack(rows_p, 0), stack(rows_s, 0), stack(rows_p, 1), stack(rows_s, 1),
            stack(rows_p, 2), stack(rows_s, 2), stack(rows_p, 3), stack(rows_s, 3),
            stack(rows_p, 4), stack(rows_s, 4),
            jnp.stack(sc_p), jnp.stack(sc_s), jnp.stack(cf_p), jnp.stack(cf_s))
```

```python
import functools
import math

import jax
import jax.numpy as jnp
from jax import lax
from jax.experimental import pallas as pl
from jax.experimental.pallas import tpu as pltpu

F32 = jnp.float32
BF16 = jnp.bfloat16

HEAD_DIM = 128
IDX_HEADS = 8
IDX_DIM = 64
TOPK_MAX = 256
PAGE_SIZE = 128
ROPE_THETA = 10000.0
NORM_EPS = 1e-6
LANES = 128
SUBLANES = 8
VMEM_LIMIT = 56 * 1024 * 1024
NEG_BIG = -1e30

(U_SC_IN, U_SC_B, U_SC_C, U_DQ, U_DK, U_DV, U_SQ, U_SK, U_SV, U_IDX, U_CF_A, U_CF_B) = range(12)
N_UBLOCKS = 12


def _cparams(n_axes):
    return pltpu.CompilerParams(dimension_semantics=("arbitrary",) * n_axes,
                                vmem_limit_bytes=VMEM_LIMIT)


def _dot(a, b):
    return jnp.dot(a, b, preferred_element_type=F32)


def _dot_nt(a, b):
    return lax.dot_general(a, b, (((1,), (1,)), ((), ())), preferred_element_type=F32)


def _sigmoid(x):
    return 1.0 / (1.0 + jnp.exp(-x))


def _pick_tile(n, candidates):
    for c in candidates:
        if n % c == 0:
            return c
    return n


def _rmsnorm_kernel(x_ref, g_ref, o_ref):
    x = x_ref[...]
    y = x * lax.rsqrt(jnp.mean(x * x, axis=-1, keepdims=True) + NORM_EPS)
    o_ref[...] = (y * g_ref[...]).astype(o_ref.dtype)


def _rmsnorm(x, g):
    m, d = x.shape
    tm = _pick_tile(m, (256, 128, 64, 32, 16, 8))
    return pl.pallas_call(
        _rmsnorm_kernel,
        grid=(m // tm,),
        in_specs=[pl.BlockSpec((tm, d), lambda i: (i, 0)),
                  pl.BlockSpec((1, d), lambda i: (0, 0))],
        out_specs=pl.BlockSpec((tm, d), lambda i: (i, 0)),
        out_shape=jax.ShapeDtypeStruct((m, d), BF16),
        compiler_params=_cparams(1),
        name="rmsnorm",
    )(x, g.reshape(1, d))


def _mm_kernel(x_ref, w_ref, o_ref):
    o_ref[...] = _dot(x_ref[...], w_ref[...]).astype(o_ref.dtype)


def _matmul(x, w, layer, tm, tn, out_dtype):
    m, k = x.shape
    n = w.shape[-1]
    return pl.pallas_call(
        _mm_kernel,
        grid=(m // tm, n // tn),
        in_specs=[pl.BlockSpec((tm, k), lambda i, j: (i, 0)),
                  pl.BlockSpec((None, k, tn), lambda i, j: (layer, 0, j))],
        out_specs=pl.BlockSpec((tm, tn), lambda i, j: (i, j)),
        out_shape=jax.ShapeDtypeStruct((m, n), out_dtype),
        compiler_params=_cparams(2),
        name="in_proj",
    )(x, w)


def _merge_kernel(h_ref, y_ref, wg_ref, wb_ref, o_ref, acc_ref):
    n = pl.program_id(2)
    gate = _sigmoid(_dot(h_ref[...], wg_ref[...]))
    val = gate * _dot(y_ref[...], wb_ref[...])

    @pl.when(n == 0)
    def _():
        acc_ref[...] = val

    @pl.when(n > 0)
    def _():
        acc_ref[...] += val

    @pl.when(n == pl.num_programs(2) - 1)
    def _():
        o_ref[...] = acc_ref[...].astype(o_ref.dtype)


def _merge(h, y, wg, wb, layer, tm, tn):
    m, d = h.shape
    nb, _, bw = y.shape
    return pl.pallas_call(
        _merge_kernel,
        grid=(m // tm, d // tn, nb),
        in_specs=[pl.BlockSpec((tm, d), lambda i, j, n: (i, 0)),
                  pl.BlockSpec((None, tm, bw), lambda i, j, n: (n, i, 0)),
                  pl.BlockSpec((None, None, d, tn), lambda i, j, n: (layer, n, 0, j)),
                  pl.BlockSpec((None, None, bw, tn), lambda i, j, n: (layer, n, 0, j))],
        out_specs=pl.BlockSpec((tm, tn), lambda i, j, n: (i, j)),
        out_shape=jax.ShapeDtypeStruct((m, d), BF16),
        scratch_shapes=[pltpu.VMEM((tm, tn), F32)],
        compiler_params=_cparams(3),
        name="gated_merge",
    )(h, y, wg, wb)


def _resid_mm_kernel(x_ref, a_ref, w_ref, o_ref):
    o_ref[...] = x_ref[...] + _dot(a_ref[...], w_ref[...])


def _resid_matmul(x, a, w, layer, tm, tn, name, resident_rows=False):
    m, k = a.shape
    n = w.shape[-1]
    a_mode = dict(pipeline_mode=pl.Buffered(1)) if resident_rows else {}
    return pl.pallas_call(
        _resid_mm_kernel,
        grid=(m // tm, n // tn),
        in_specs=[pl.BlockSpec((tm, tn), lambda i, j: (i, j)),
                  pl.BlockSpec((tm, k), lambda i, j: (i, 0), **a_mode),
                  pl.BlockSpec((None, k, tn), lambda i, j: (layer, 0, j))],
        out_specs=pl.BlockSpec((tm, tn), lambda i, j: (i, j)),
        out_shape=jax.ShapeDtypeStruct((m, n), F32),
        compiler_params=_cparams(2),
        name=name,
    )(x, a, w)


def _ffn_up_kernel(h_ref, wg_ref, wu_ref, o_ref):
    h = h_ref[...]
    g = _dot(h, wg_ref[...].astype(BF16))
    u = _dot(h, wu_ref[...].astype(BF16))
    o_ref[...] = (g * _sigmoid(g) * u).astype(o_ref.dtype)


def _ffn_up(h, wg, wu, layer, tm, tn):
    m, k = h.shape
    n = wg.shape[-1]
    return pl.pallas_call(
        _ffn_up_kernel,
        grid=(m // tm, n // tn),
        in_specs=[pl.BlockSpec((tm, k), lambda i, j: (i, 0), pipeline_mode=pl.Buffered(1)),
                  pl.BlockSpec((None, k, tn), lambda i, j: (layer, 0, j)),
                  pl.BlockSpec((None, k, tn), lambda i, j: (layer, 0, j))],
        out_specs=pl.BlockSpec((tm, tn), lambda i, j: (i, j)),
        out_shape=jax.ShapeDtypeStruct((m, n), BF16),
        compiler_params=_cparams(2),
        name="ffn_up",
    )(h, wg, wu)


def _sconv_kernel(cin_ref, cb_ref, cc_ref, st_ref, w_ref, y_ref, ns_ref, ext_ref, *, tt, width):
    t = pl.program_id(1)
    hist = width - 1
    base = SUBLANES

    @pl.when(t == 0)
    def _():
        ext_ref[base - hist:base, :] = st_ref[...]

    @pl.when(t > 0)
    def _():
        ext_ref[base - hist:base, :] = ext_ref[base + tt - hist:base + tt, :]

    ext_ref[base:base + tt, :] = cc_ref[...] * cin_ref[...]
    z = None
    for k in range(width):
        term = w_ref[k:k + 1, :] * ext_ref[base - hist + k:base - hist + k + tt, :]
        z = term if z is None else z + term
    y_ref[...] = (cb_ref[...] * z).astype(y_ref.dtype)

    @pl.when(t == pl.num_programs(1) - 1)
    def _():
        ns_ref[...] = ext_ref[base + tt - hist:base + tt, :]


def _sconv(u3, state, w, tt):
    b, t, _ = u3.shape
    width, bw = w.shape
    assert width - 1 <= SUBLANES and (tt >= width - 1 or t == tt)

    def ublock(c):
        return pl.BlockSpec((None, tt, bw), lambda bi, ti, c=c: (bi, ti, c))

    return pl.pallas_call(
        functools.partial(_sconv_kernel, tt=tt, width=width),
        grid=(b, t // tt),
        in_specs=[ublock(U_SC_IN), ublock(U_SC_B), ublock(U_SC_C),
                  pl.BlockSpec((None, width - 1, bw), lambda bi, ti: (bi, 0, 0)),
                  pl.BlockSpec((width, bw), lambda bi, ti: (0, 0))],
        out_specs=[pl.BlockSpec((None, tt, bw), lambda bi, ti: (bi, ti, 0)),
                   pl.BlockSpec((None, width - 1, bw), lambda bi, ti: (bi, 0, 0))],
        out_shape=[jax.ShapeDtypeStruct((b, t, bw), BF16),
                   jax.ShapeDtypeStruct((b, width - 1, bw), F32)],
        scratch_shapes=[pltpu.VMEM((SUBLANES + tt, bw), F32)],
        compiler_params=_cparams(2),
        name="short_conv",
    )(u3, u3, u3, state, w)


def _cconv_kernel(ca_ref, cb_ref, st_ref, w_ref, bias_ref, ln_ref, y_ref, ns_ref, ext_ref, c_ref,
                  *, tt, width, base):
    t = pl.program_id(1)
    hist = width - 1
    bw = ca_ref.shape[-1]

    @pl.when(t == 0)
    def _():
        ext_ref[base - hist:base, :] = st_ref[...]

    @pl.when(t > 0)
    def _():
        ext_ref[base - hist:base, :] = ext_ref[base + tt - hist:base + tt, :]

    ext_ref[base:base + tt, :] = ca_ref[...] * _sigmoid(cb_ref[...])
    for j in range(bw // LANES):
        ls = slice(j * LANES, (j + 1) * LANES)
        acc = None
        for k in range(width):
            term = w_ref[k:k + 1, ls] * ext_ref[base - hist + k:base - hist + k + tt, ls]
            acc = term if acc is None else acc + term
        c_ref[:, ls] = acc + bias_ref[:, ls]
    c = c_ref[...]
    xc = c - jnp.mean(c, axis=-1, keepdims=True)
    yn = xc * lax.rsqrt(jnp.mean(xc * xc, axis=-1, keepdims=True) + NORM_EPS)
    yn = yn * ln_ref[0:1, :] + ln_ref[1:2, :]
    y_ref[...] = (yn * _sigmoid(yn)).astype(y_ref.dtype)

    @pl.when(t == pl.num_programs(1) - 1)
    def _():
        ns_ref[...] = ext_ref[base + tt - hist:base + tt, :]


def _cconv(u3, state, w, bias, ln, tt):
    b, t, _ = u3.shape
    width, bw = w.shape
    hist = width - 1
    base = -(-hist // SUBLANES) * SUBLANES
    assert tt >= hist or t == tt

    def ublock(c):
        return pl.BlockSpec((None, tt, bw), lambda bi, ti, c=c: (bi, ti, c))

    return pl.pallas_call(
        functools.partial(_cconv_kernel, tt=tt, width=width, base=base),
        grid=(b, t // tt),
        in_specs=[ublock(U_CF_A), ublock(U_CF_B),
                  pl.BlockSpec((None, hist, bw), lambda bi, ti: (bi, 0, 0)),
                  pl.BlockSpec((width, bw), lambda bi, ti: (0, 0)),
                  pl.BlockSpec((1, bw), lambda bi, ti: (0, 0)),
                  pl.BlockSpec((2, bw), lambda bi, ti: (0, 0))],
        out_specs=[pl.BlockSpec((None, tt, bw), lambda bi, ti: (bi, ti, 0)),
                   pl.BlockSpec((None, hist, bw), lambda bi, ti: (bi, 0, 0))],
        out_shape=[jax.ShapeDtypeStruct((b, t, bw), BF16),
                   jax.ShapeDtypeStruct((b, hist, bw), F32)],
        scratch_shapes=[pltpu.VMEM((base + tt, bw), F32), pltpu.VMEM((tt, bw), F32)],
        compiler_params=_cparams(2),
        name="conformer_conv",
    )(u3, u3, state, w, bias.reshape(1, bw), ln)


def _qkprep_kernel(dq_ref, dk_ref, dv_ref, sq_ref, sk_ref, sv_ref, idx_ref, g_ref,
                   cos_ref, sin_ref, cos64_ref, sina_ref, sinb_ref,
                   dqb_ref, dkf_ref, dkb_ref, dvb_ref, sqb_ref, skf_ref, skb_ref, svb_ref,
                   iqb_ref, ikw_ref, iklo_ref, ikhi_ref):
    cos = cos_ref[...]
    sin = sin_ref[...]
    n_heads = dq_ref.shape[-1] // HEAD_DIM

    def norm_rope(x, g):
        y = x * lax.rsqrt(jnp.mean(x * x, axis=-1, keepdims=True) + NORM_EPS) * g
        return y * cos + pltpu.roll(y, HEAD_DIM // 2, 1) * sin

    for h in range(n_heads):
        ls = slice(h * HEAD_DIM, (h + 1) * HEAD_DIM)
        dqb_ref[:, ls] = norm_rope(dq_ref[:, ls], g_ref[0:1, :]).astype(BF16)
        dk = norm_rope(dk_ref[:, ls], g_ref[1:2, :])
        dkf_ref[:, ls] = dk
        dkb_ref[:, ls] = dk.astype(BF16)
        sqb_ref[:, ls] = norm_rope(sq_ref[:, ls], g_ref[2:3, :]).astype(BF16)
        sk = norm_rope(sk_ref[:, ls], g_ref[3:4, :])
        skf_ref[:, ls] = sk
        skb_ref[:, ls] = sk.astype(BF16)
    dvb_ref[...] = dv_ref[...].astype(BF16)
    svb_ref[...] = sv_ref[...].astype(BF16)

    cos64 = cos64_ref[...]
    sina = sina_ref[...]
    sinb = sinb_ref[...]

    def rope64(x):
        return (x * cos64 + pltpu.roll(x, LANES - IDX_DIM // 2, 1) * sina
                + pltpu.roll(x, IDX_DIM // 2, 1) * sinb)

    n_iq = IDX_HEADS * IDX_DIM
    for j in range(n_iq // LANES):
        ls = slice(j * LANES, (j + 1) * LANES)
        iqb_ref[:, ls] = rope64(idx_ref[:, ls]).astype(BF16)
    xs = idx_ref[:, n_iq:n_iq + LANES]
    lane = lax.broadcasted_iota(jnp.int32, xs.shape, 1)
    ik = jnp.where(lane < IDX_DIM, rope64(xs), 0.0)
    iw = jnp.where((lane >= IDX_DIM) & (lane < IDX_DIM + IDX_HEADS), xs * (IDX_HEADS ** -0.5), 0.0)
    ikw_ref[...] = ik + iw
    iklo_ref[...] = ik.astype(BF16)
    ikhi_ref[...] = pltpu.roll(ik, IDX_DIM, 1).astype(BF16)


def _qkprep(u, qk_norm, tabs, tt, n_tblocks):
    m = u.shape[0]
    bw = u.shape[1] // N_UBLOCKS

    def ublock(c):
        return pl.BlockSpec((tt, bw), lambda i, c=c: (i, c))

    def tab():
        return pl.BlockSpec((tt, LANES), lambda i: (i % n_tblocks, 0))

    def out(w):
        return pl.BlockSpec((tt, w), lambda i: (i, 0))

    n_iq = IDX_HEADS * IDX_DIM
    shapes = [(bw, BF16), (bw, F32), (bw, BF16), (bw, BF16), (bw, BF16), (bw, F32), (bw, BF16), (bw, BF16),
              (n_iq, BF16), (LANES, F32), (LANES, BF16), (LANES, BF16)]
    return pl.pallas_call(
        _qkprep_kernel,
        grid=(m // tt,),
        in_specs=[ublock(U_DQ), ublock(U_DK), ublock(U_DV), ublock(U_SQ), ublock(U_SK), ublock(U_SV),
                  ublock(U_IDX), pl.BlockSpec((4, HEAD_DIM), lambda i: (0, 0)),
                  tab(), tab(), tab(), tab(), tab()],
        out_specs=[out(w) for w, _ in shapes],
        out_shape=[jax.ShapeDtypeStruct((m, w), dt) for w, dt in shapes],
        compiler_params=_cparams(1),
        name="qk_norm_rope",
    )(u, u, u, u, u, u, u, qk_norm, *tabs)


def _rope_tables(pos):
    def angles(d):
        inv = ROPE_THETA ** (-jnp.arange(0, d, 2, dtype=F32) / d)
        return pos.astype(F32)[:, None] * inv[None, :]

    a = angles(HEAD_DIM)
    cos = jnp.concatenate([jnp.cos(a), jnp.cos(a)], axis=1)
    sin = jnp.concatenate([-jnp.sin(a), jnp.sin(a)], axis=1)
    a2 = angles(IDX_DIM)
    c2, s2, z2 = jnp.cos(a2), jnp.sin(a2), jnp.zeros_like(a2)
    cos64 = jnp.concatenate([c2, c2, c2, c2], axis=1)
    sina = jnp.concatenate([-s2, z2, -s2, z2], axis=1)
    sinb = jnp.concatenate([z2, s2, z2, s2], axis=1)
    return (cos, sin, cos64, sina, sinb)


def _diff_lambda(dl_ref, lam_init):
    lv = dl_ref[...]
    a = jnp.sum(lv[0:1, :] * lv[1:2, :], axis=-1, keepdims=True)
    b = jnp.sum(lv[2:3, :] * lv[3:4, :], axis=-1, keepdims=True)
    return jnp.exp(a) - jnp.exp(b) + lam_init


def _subln(o, g, lam_init):
    y = o * lax.rsqrt(jnp.mean(o * o, axis=-1, keepdims=True) + NORM_EPS)
    return (y * g) * (1.0 - lam_init)


def _topk_select(score, col, k, idx_bits):
    r = score.shape[0]
    bits = pltpu.bitcast(score + 0.0, jnp.int32)
    key = jnp.where(bits < 0, bits ^ jnp.int32(0x7FFFFFFF), bits)
    kf = float(k)

    def count(pred):
        return jnp.sum(jnp.where(pred, 1.0, 0.0), axis=-1, keepdims=True)

    zero = jnp.zeros((r, 1), jnp.int32)
    start = jnp.where(count(key >= zero) >= kf, zero, jnp.full((r, 1), -2 ** 31, jnp.int32))

    def value_step(i, cur):
        cand = cur + jnp.left_shift(jnp.int32(1), 30 - i)
        return jnp.where(count(key >= cand) >= kf, cand, cur)

    thr = lax.fori_loop(0, 31, value_step, start)
    above = key > thr
    tie = key == thr
    need = kf - count(above)

    def index_step(i, cur):
        cand = cur + jnp.left_shift(jnp.int32(1), idx_bits - 1 - i)
        cnt = jnp.sum(jnp.where(tie, jnp.where(col < cand, 1.0, 0.0), 0.0), axis=-1, keepdims=True)
        return jnp.where(cnt <= need, cand, cur)

    bound = lax.fori_loop(0, idx_bits, index_step, zero)
    return above | (tie & (col < bound))


def _diff_attn_kernel(q_ref, k_ref, v_ref, dl_ref, g_ref, o_ref, *, tq, q0, lam_init):
    qi = pl.program_id(2)
    q = q_ref[...]
    k = k_ref[...]
    t = k.shape[0]
    row = q0 + qi * tq + lax.broadcasted_iota(jnp.int32, (tq, t), 0)
    col = lax.broadcasted_iota(jnp.int32, (tq, t), 1)
    mask = col <= row
    scale = HEAD_DIM ** -0.5

    def softmax(c):
        ls = slice(c * HEAD_DIM, (c + 1) * HEAD_DIM)
        s = jnp.where(mask, _dot_nt(q[:, ls], k[:, ls]) * scale, -jnp.inf)
        e = jnp.exp(s - jnp.max(s, axis=-1, keepdims=True))
        return e * (1.0 / jnp.sum(e, axis=-1, keepdims=True))

    lam = _diff_lambda(dl_ref, lam_init)
    p = softmax(0) - lam * softmax(1)
    o = _dot(p.astype(BF16), v_ref[...])
    o_ref[...] = _subln(o, g_ref[...], lam_init).astype(o_ref.dtype)


def _causal_groups(t, tq):
    n = _pick_tile(t // tq, (8, 4, 2, 1))
    tg = t // n
    return [(g * tg, tg) for g in range(n)]


def _diff_attn_prompt(dq, dk, dv, diff_lambda, subln, b, t, lam_init, tq):
    bw = dq.shape[-1]
    vd = 2 * HEAD_DIM
    nh = bw // vd
    q3, k3, v3 = (a.reshape(b, t, bw) for a in (dq, dk, dv))
    outs = []
    for q0, tg in _causal_groups(t, tq):
        tk = q0 + tg
        qb0 = q0 // tq
        outs.append(pl.pallas_call(
            functools.partial(_diff_attn_kernel, tq=tq, q0=q0, lam_init=lam_init),
            grid=(b, nh, tg // tq),
            in_specs=[pl.BlockSpec((None, tq, vd), lambda bi, h, qi, qb0=qb0: (bi, qb0 + qi, h)),
                      pl.BlockSpec((None, tk, vd), lambda bi, h, qi: (bi, 0, h)),
                      pl.BlockSpec((None, tk, vd), lambda bi, h, qi: (bi, 0, h)),
                      pl.BlockSpec((4, HEAD_DIM), lambda bi, h, qi: (0, 0)),
                      pl.BlockSpec((1, vd), lambda bi, h, qi: (0, 0))],
            out_specs=pl.BlockSpec((None, tq, vd), lambda bi, h, qi: (bi, qi, h)),
            out_shape=jax.ShapeDtypeStruct((b, tg, bw), BF16),
            compiler_params=_cparams(3),
            name="diff_attn_prompt",
        )(q3, k3, v3, diff_lambda, subln.reshape(1, vd)))
    return jnp.concatenate(outs, axis=1)


def _dsa_attn_kernel(q_ref, k_ref, v_ref, iq_ref, iklo_ref, ikhi_ref, iw_ref, o_ref, *, tq, q0, topk,
                     idx_bits):
    qi = pl.program_id(1)
    t = k_ref.shape[0]
    iklo = iklo_ref[...]
    ikhi = ikhi_ref[...]
    iw = iw_ref[...]
    score = None
    for j in range(IDX_HEADS // 2):
        qpair = iq_ref[:, j * LANES:(j + 1) * LANES]
        for half, ik in enumerate((iklo, ikhi)):
            h = 2 * j + half
            sc = jnp.maximum(_dot_nt(qpair, ik) * (IDX_DIM ** -0.5), 0.0)
            term = iw[:, IDX_DIM + h:IDX_DIM + h + 1] * sc
            score = term if score is None else score + term
    row = q0 + qi * tq + lax.broadcasted_iota(jnp.int32, (tq, t), 0)
    col = lax.broadcasted_iota(jnp.int32, (tq, t), 1)
    causal = col <= row
    sel = _topk_select(jnp.where(causal, score, -jnp.inf), col, topk, idx_bits)
    keep = causal & sel
    scale = HEAD_DIM ** -0.5
    for h in range(q_ref.shape[-1] // HEAD_DIM):
        ls = slice(h * HEAD_DIM, (h + 1) * HEAD_DIM)
        s = jnp.where(keep, _dot_nt(q_ref[:, ls], k_ref[:, ls]) * scale, -jnp.inf)
        e = jnp.exp(s - jnp.max(s, axis=-1, keepdims=True))
        a = e * (1.0 / jnp.sum(e, axis=-1, keepdims=True))
        o_ref[:, ls] = _dot(a.astype(BF16), v_ref[:, ls]).astype(o_ref.dtype)


def _dsa_attn_prompt(sq, sk, sv, iq, iklo, ikhi, ikw, b, t, tq):
    bw = sq.shape[-1]
    n_iq = iq.shape[-1]
    topk = min(TOPK_MAX, t // 4)
    args = (sq.reshape(b, t, bw), sk.reshape(b, t, bw), sv.reshape(b, t, bw), iq.reshape(b, t, n_iq),
            iklo.reshape(b, t, LANES), ikhi.reshape(b, t, LANES), ikw.reshape(b, t, LANES))
    outs = []
    for q0, tg in _causal_groups(t, tq):
        tk = q0 + tg
        qb0 = q0 // tq

        def qblock(w, qb0=qb0):
            return pl.BlockSpec((None, tq, w), lambda bi, qi: (bi, qb0 + qi, 0))

        def kblock(w, tk=tk):
            return pl.BlockSpec((None, tk, w), lambda bi, qi: (bi, 0, 0))

        outs.append(pl.pallas_call(
            functools.partial(_dsa_attn_kernel, tq=tq, q0=q0, topk=topk, idx_bits=int(tk).bit_length()),
            grid=(b, tg // tq),
            in_specs=[qblock(bw), kblock(bw), kblock(bw), qblock(n_iq), kblock(LANES), kblock(LANES),
                      qblock(LANES)],
            out_specs=pl.BlockSpec((None, tq, bw), lambda bi, qi: (bi, qi, 0)),
            out_shape=jax.ShapeDtypeStruct((b, tg, bw), BF16),
            compiler_params=_cparams(2),
            name="dsa_attn_prompt",
        )(*args))
    return jnp.concatenate(outs, axis=1)


GROUPS = SUBLANES
PAGE_ROWS = PAGE_SIZE * GROUPS


def _pages_per_step(n_pages):
    return _pick_tile(n_pages, (8, 4, 2, 1))


def _page_specs(layer, npp, n_steps, rows, width):
    def spec(r):
        def index(bi, p, pt):
            return (layer, pt[bi, jnp.minimum(p, n_steps - 1) * npp + r], 0, 0)
        return pl.BlockSpec((None, None, rows, width), index)
    return [spec(r) for r in range(npp)]


def _group_match(shape):
    lane = lax.broadcasted_iota(jnp.int32, shape, 1)
    row = lax.broadcasted_iota(jnp.int32, shape, 0)
    return (lane & (GROUPS - 1)) == (row >> 3)


def _softmax_update(s, keep, m_ref, l_ref):
    s = jnp.where(keep, s, NEG_BIG)
    m_old = m_ref[...]
    m_new = jnp.maximum(m_old, jnp.max(s, axis=-1, keepdims=True))
    alpha = jnp.exp(m_old - m_new)
    p = jnp.where(keep, jnp.exp(s - m_new), 0.0)
    l_ref[...] = alpha * l_ref[...] + jnp.sum(p, axis=-1, keepdims=True)
    m_ref[...] = m_new
    return p, alpha


def _softmax_update_biased(s, m_ref, l_ref):
    m_old = m_ref[...]
    m_new = jnp.maximum(m_old, jnp.max(s, axis=-1, keepdims=True))
    alpha = jnp.exp(m_old - m_new)
    p = jnp.exp(s - m_new)
    l_ref[...] = alpha * l_ref[...] + jnp.sum(p, axis=-1, keepdims=True)
    m_ref[...] = m_new
    return p, alpha


def _group_bias(shape):
    return jnp.where(_group_match(shape), 0.0, NEG_BIG)


def _init_softmax_state(m_ref, l_ref, acc_ref):
    m_ref[...] = jnp.full(m_ref.shape, NEG_BIG, F32)
    l_ref[...] = jnp.zeros(l_ref.shape, F32)
    acc_ref[...] = jnp.zeros(acc_ref.shape, F32)


def _new_key_mask(shape, t_new):
    key = lax.broadcasted_iota(jnp.int32, shape, 1) >> 3
    qrow = lax.broadcasted_iota(jnp.int32, shape, 0) & (SUBLANES - 1)
    return _group_match(shape) & (key <= qrow) & (key < t_new)


def _value_row_shift(h, c, half, n_heads):
    return (half * n_heads + h) - (2 * h + c)


def _spread_weights(p, n_heads):
    n = p.shape[1]
    tiles = []
    for h in range(n_heads):
        for c in range(2):
            tile = p[(2 * h + c) * SUBLANES:(2 * h + c + 1) * SUBLANES, :]
            for half in range(2):
                shift = _value_row_shift(h, c, half, n_heads) % n
                tiles.append(tile if shift == 0 else pltpu.roll(tile, shift, 1))
    return jnp.concatenate(tiles, axis=0)


def _spread_rows(a):
    tiles = []
    for g in range(GROUPS):
        tile = a[g * SUBLANES:(g + 1) * SUBLANES, :]
        tiles += [tile, tile]
    return jnp.concatenate(tiles, axis=0)


def _diff_attn_sample_kernel(pt_ref, q_ref, *refs, npp, n_steps, t_new, lam_init):
    k_refs = refs[:npp]
    v_refs = refs[npp:2 * npp]
    kn_ref, vn_ref, dl_ref, g_ref, o_ref, m_ref, l_ref, acc_ref, bias_ref = refs[2 * npp:]
    p = pl.program_id(1)
    n_heads = GROUPS // 2
    scale = HEAD_DIM ** -0.5

    @pl.when(p == 0)
    def _():
        _init_softmax_state(m_ref, l_ref, acc_ref)
        bias_ref[...] = _group_bias(bias_ref.shape)

    def accumulate(w, alpha, v_parts):
        pv = None
        for r, v in enumerate(v_parts):
            rows = v.shape[0]
            part = _dot(_spread_weights(w[:, r * rows:(r + 1) * rows], n_heads).astype(BF16), v)
            pv = part if pv is None else pv + part
        acc_ref[...] = _spread_rows(alpha) * acc_ref[...] + pv

    @pl.when(p < n_steps)
    def _():
        q = q_ref[...]
        s = jnp.concatenate([_dot_nt(q, kr[...].astype(BF16)) for kr in k_refs], axis=1) * scale
        w, alpha = _softmax_update_biased(s + bias_ref[...], m_ref, l_ref)
        accumulate(w, alpha, [vr[...].astype(BF16) for vr in v_refs])

    @pl.when(p == n_steps)
    def _():
        s = _dot_nt(q_ref[...], kn_ref[...]) * scale
        w, alpha = _softmax_update(s, _new_key_mask(s.shape, t_new), m_ref, l_ref)
        accumulate(w, alpha, [vn_ref[...]])
        out = acc_ref[...] * _spread_rows(1.0 / l_ref[...])
        lam = _diff_lambda(dl_ref, lam_init)
        vd = 2 * HEAD_DIM
        for h in range(n_heads):
            def tile(c, half):
                r0 = ((2 * h + c) * 2 + half) * SUBLANES
                return out[r0:r0 + SUBLANES, :]
            o = jnp.concatenate([tile(0, half) - lam * tile(1, half) for half in range(2)], axis=1)
            o_ref[:, h * vd:(h + 1) * vd] = _subln(o, g_ref[...], lam_init).astype(o_ref.dtype)


def _group_major_queries(q, b, t):
    qg = q.reshape(b, t, GROUPS, HEAD_DIM).transpose(0, 2, 1, 3)
    qg = jnp.pad(qg, ((0, 0), (0, 0), (0, SUBLANES - t), (0, 0)))
    return qg.reshape(b, GROUPS * SUBLANES, HEAD_DIM)


def _pad_rows(a, rows):
    return jnp.pad(a, ((0, 0), (0, rows - a.shape[1]), (0, 0)))


def _value_rows(v, lead):
    n_heads = GROUPS // 2
    v = v.reshape(lead + (-1, n_heads, 2, HEAD_DIM))
    v = jnp.swapaxes(v, -3, -2)
    return v.reshape(lead + (-1, HEAD_DIM))


def _diff_attn_sample(dq, dk, dv, cache_k, cache_v, layer, page_table, diff_lambda, subln, b, t, lam_init):
    bw = dq.shape[-1]
    vd = 2 * HEAD_DIM
    n_pages = page_table.shape[1]
    npp = _pages_per_step(n_pages)
    n_steps = n_pages // npp
    rows = GROUPS * SUBLANES
    new_rows = LANES
    qg = _group_major_queries(dq, b, t)
    kn = _pad_rows(dk.reshape(b, t * GROUPS, HEAD_DIM), new_rows)
    vn = _pad_rows(_value_rows(dv.reshape(b, t, GROUPS // 2, vd), (b,)), new_rows)

    def per_batch(r, w):
        return pl.BlockSpec((None, r, w), lambda bi, p, pt: (bi, 0, 0))

    grid_spec = pltpu.PrefetchScalarGridSpec(
        num_scalar_prefetch=1,
        grid=(b, n_steps + 1),
        in_specs=[per_batch(rows, HEAD_DIM)]
        + _page_specs(layer, npp, n_steps, PAGE_ROWS, HEAD_DIM)
        + _page_specs(layer, npp, n_steps, PAGE_ROWS, HEAD_DIM)
        + [per_batch(new_rows, HEAD_DIM), per_batch(new_rows, HEAD_DIM),
           pl.BlockSpec((4, HEAD_DIM), lambda bi, p, pt: (0, 0)),
           pl.BlockSpec((1, vd), lambda bi, p, pt: (0, 0))],
        out_specs=per_batch(SUBLANES, bw),
        scratch_shapes=[pltpu.VMEM((rows, 1), F32), pltpu.VMEM((rows, 1), F32),
                        pltpu.VMEM((2 * rows, HEAD_DIM), F32), pltpu.VMEM((rows, npp * PAGE_ROWS), F32)],
    )
    out = pl.pallas_call(
        functools.partial(_diff_attn_sample_kernel, npp=npp, n_steps=n_steps, t_new=t, lam_init=lam_init),
        grid_spec=grid_spec,
        out_shape=jax.ShapeDtypeStruct((b, SUBLANES, bw), BF16),
        compiler_params=_cparams(2),
        name="diff_attn_sample",
    )(page_table, qg, *([cache_k] * npp), *([cache_v] * npp), kn, vn, diff_lambda, subln.reshape(1, vd))
    return out[:, :t].reshape(b * t, bw)


def _idx_score_kernel(pt_ref, q_ref, w_ref, *refs, npp, n_steps):
    k_refs = refs[:npp]
    kn_ref, o_ref = refs[npp:]
    p = pl.program_id(1)
    q = q_ref[...]
    w = w_ref[...]

    def scores(kt):
        sc = jnp.maximum(_dot(q, kt) * (IDX_DIM ** -0.5), 0.0) * w
        out = sc[0:SUBLANES]
        for h in range(1, IDX_HEADS):
            out = out + sc[h * SUBLANES:(h + 1) * SUBLANES]
        return out

    @pl.when(p < n_steps)
    def _():
        for r, kr in enumerate(k_refs):
            o_ref[:, r * PAGE_SIZE:(r + 1) * PAGE_SIZE] = scores(kr[...].astype(BF16))

    @pl.when(p == n_steps)
    def _():
        o_ref[...] = scores(kn_ref[...])


def _select_sample_kernel(s_ref, o_ref, *, past, t_new, topk, idx_bits):
    s = s_ref[...]
    qrow = jnp.minimum(lax.broadcasted_iota(jnp.int32, s.shape, 0), t_new - 1)
    col = lax.broadcasted_iota(jnp.int32, s.shape, 1)
    admissible = (col <= past + qrow) & (col < past + t_new)
    sel = _topk_select(jnp.where(admissible, s, -jnp.inf), col, topk, idx_bits)
    o_ref[...] = jnp.where(admissible, jnp.where(sel, 1.0, 0.0), 0.0)


def _dsa_attn_sample_kernel(pt_ref, q_ref, keep_ref, spread_ref, *refs, npp, n_steps):
    k_refs = refs[:npp]
    v_refs = refs[npp:2 * npp]
    kn_ref, vn_ref, o_ref, m_ref, l_ref, acc_ref, bias_ref = refs[2 * npp:]
    p = pl.program_id(1)
    scale = HEAD_DIM ** -0.5

    @pl.when(p == 0)
    def _():
        _init_softmax_state(m_ref, l_ref, acc_ref)
        bias_ref[...] = _group_bias(bias_ref.shape)

    def keep_lanes(n_pages):
        spread = spread_ref[...]
        parts = [_dot(keep_ref[:, r * PAGE_SIZE:(r + 1) * PAGE_SIZE].astype(BF16), spread)
                 for r in range(n_pages)]
        return parts[0] if n_pages == 1 else jnp.concatenate(parts, axis=1)

    def keep_mask(n):
        kp = keep_lanes(1)[:, 0:n] > 0.5
        return _group_match((GROUPS * SUBLANES, n)) & jnp.concatenate([kp] * GROUPS, axis=0)

    def accumulate(w, alpha, v_parts):
        pv = None
        for r, v in enumerate(v_parts):
            rows = v.shape[0]
            part = _dot(w[:, r * rows:(r + 1) * rows].astype(BF16), v)
            pv = part if pv is None else pv + part
        acc_ref[...] = alpha * acc_ref[...] + pv

    @pl.when(p < n_steps)
    def _():
        q = q_ref[...]
        s = jnp.concatenate([_dot_nt(q, kr[...].astype(BF16)) for kr in k_refs], axis=1) * scale
        keep_bias = (keep_lanes(npp) - 1.0) * (-NEG_BIG)
        s = s + bias_ref[...] + jnp.concatenate([keep_bias] * GROUPS, axis=0)
        w, alpha = _softmax_update_biased(s, m_ref, l_ref)
        accumulate(w, alpha, [vr[...].astype(BF16) for vr in v_refs])

    @pl.when(p == n_steps)
    def _():
        s = _dot_nt(q_ref[...], kn_ref[...]) * scale
        w, alpha = _softmax_update(s, keep_mask(s.shape[1]), m_ref, l_ref)
        accumulate(w, alpha, [vn_ref[...]])
        o_ref[...] = (acc_ref[...] * (1.0 / l_ref[...])).astype(o_ref.dtype)


def _dsa_attn_sample(sq, sk, sv, iq, ikb, ikw, cache_k, cache_v, cache_kidx, layer, page_table, b, t):
    bw = sq.shape[-1]
    n_pages = page_table.shape[1]
    past = n_pages * PAGE_SIZE
    npp = _pages_per_step(n_pages)
    n_steps = n_pages // npp
    step_w = npp * PAGE_SIZE
    s_pad = (n_steps + 1) * step_w
    topk = min(TOPK_MAX, (past + t) // 4)
    idx_bits = int(s_pad).bit_length()

    def per_batch(r, w):
        return pl.BlockSpec((None, r, w), lambda bi, p, pt: (bi, 0, 0))

    n_iq = iq.shape[-1]
    qi = iq.reshape(b, t, IDX_HEADS, IDX_DIM).transpose(0, 2, 1, 3)
    qi = jnp.pad(qi, ((0, 0), (0, 0), (0, SUBLANES - t), (0, 0))).reshape(b, IDX_HEADS * SUBLANES, IDX_DIM)
    wi = ikw.reshape(b, t, LANES)[:, :, IDX_DIM:IDX_DIM + IDX_HEADS].transpose(0, 2, 1)
    wi = jnp.pad(wi, ((0, 0), (0, 0), (0, SUBLANES - t))).reshape(b, IDX_HEADS * SUBLANES, 1)
    kin = jnp.swapaxes(_pad_rows(ikb.reshape(b, t, LANES)[:, :, :IDX_DIM], step_w), 1, 2)
    scores = pl.pallas_call(
        functools.partial(_idx_score_kernel, npp=npp, n_steps=n_steps),
        grid_spec=pltpu.PrefetchScalarGridSpec(
            num_scalar_prefetch=1,
            grid=(b, n_steps + 1),
            in_specs=[per_batch(IDX_HEADS * SUBLANES, IDX_DIM), per_batch(IDX_HEADS * SUBLANES, 1)]
            + _page_specs(layer, npp, n_steps, IDX_DIM, PAGE_SIZE) + [per_batch(IDX_DIM, step_w)],
            out_specs=pl.BlockSpec((None, SUBLANES, step_w), lambda bi, p, pt: (bi, 0, p)),
        ),
        out_shape=jax.ShapeDtypeStruct((b, SUBLANES, s_pad), F32),
        compiler_params=_cparams(2),
        name="idx_score_sample",
    )(page_table, qi, wi, *([cache_kidx] * npp), kin)

    keep = pl.pallas_call(
        functools.partial(_select_sample_kernel, past=past, t_new=t, topk=topk, idx_bits=idx_bits),
        grid=(b,),
        in_specs=[pl.BlockSpec((None, SUBLANES, s_pad), lambda bi: (bi, 0, 0))],
        out_specs=pl.BlockSpec((None, SUBLANES, s_pad), lambda bi: (bi, 0, 0)),
        out_shape=jax.ShapeDtypeStruct((b, SUBLANES, s_pad), F32),
        compiler_params=_cparams(1),
        name="topk_select_sample",
    )(scores)

    rows = GROUPS * SUBLANES
    new_rows = LANES
    spread = jnp.repeat(jnp.eye(PAGE_SIZE, dtype=BF16), GROUPS, axis=1)
    qg = _group_major_queries(sq, b, t)
    kn = _pad_rows(sk.reshape(b, t * GROUPS, HEAD_DIM), new_rows)
    vn = _pad_rows(sv.reshape(b, t * GROUPS, HEAD_DIM), new_rows)
    out = pl.pallas_call(
        functools.partial(_dsa_attn_sample_kernel, npp=npp, n_steps=n_steps),
        grid_spec=pltpu.PrefetchScalarGridSpec(
            num_scalar_prefetch=1,
            grid=(b, n_steps + 1),
            in_specs=[per_batch(rows, HEAD_DIM),
                      pl.BlockSpec((None, SUBLANES, step_w), lambda bi, p, pt: (bi, 0, p)),
                      pl.BlockSpec((PAGE_SIZE, PAGE_ROWS), lambda bi, p, pt: (0, 0))]
            + _page_specs(layer, npp, n_steps, PAGE_ROWS, HEAD_DIM)
            + _page_specs(layer, npp, n_steps, PAGE_ROWS, HEAD_DIM)
            + [per_batch(new_rows, HEAD_DIM), per_batch(new_rows, HEAD_DIM)],
            out_specs=per_batch(rows, HEAD_DIM),
            scratch_shapes=[pltpu.VMEM((rows, 1), F32), pltpu.VMEM((rows, 1), F32),
                            pltpu.VMEM((rows, HEAD_DIM), F32), pltpu.VMEM((rows, npp * PAGE_ROWS), F32)],
        ),
        out_shape=jax.ShapeDtypeStruct((b, rows, HEAD_DIM), BF16),
        compiler_params=_cparams(2),
        name="dsa_attn_sample",
    )(page_table, qg, keep, spread, *([cache_k] * npp), *([cache_v] * npp), kn, vn)
    out = out.reshape(b, GROUPS, SUBLANES, HEAD_DIM)[:, :, :t].transpose(0, 2, 1, 3)
    return out.reshape(b * t, bw)


def _prep_w_in(w, bw):
    n_idx = IDX_HEADS * IDX_DIM + IDX_DIM + IDX_HEADS
    wb = w.astype(BF16)
    main = wb[..., :9 * bw]
    idx = jnp.pad(wb[..., 9 * bw:9 * bw + n_idx], ((0, 0), (0, 0), (0, bw - n_idx)))
    return jnp.concatenate([main, idx, wb[..., 9 * bw + n_idx:]], axis=-1)


def _layer(x, b, t, pos, sc_state, cf_state, attend_diff, attend_dsa, w, tiles):
    m, d = x.shape
    bw = d // 4
    h = _rmsnorm(x, w["norm_mix"])
    layer = w["layer"]
    u = _matmul(h, w["w_in"], layer, tiles["tm"], tiles["tn_in"], F32)
    u3 = u.reshape(b, t, N_UBLOCKS * bw)

    y_a, sc_new = _sconv(u3, sc_state, w["conv_sc"], tiles["tt_conv"])
    y_d, cf_new = _cconv(u3, cf_state, w["conv_conf"], w["conv_conf_bias"], w["conf_ln"], tiles["tt_conv"])

    tabs = _rope_tables(pos)
    if tiles["tt_prep"] > t:
        tabs = tuple(jnp.tile(a, (tiles["tt_prep"] // t, 1)) for a in tabs)
    n_tblocks = max(1, t // tiles["tt_prep"])
    (dqb, dkf, dkb, dvb, sqb, skf, skb, svb, iqb, ikw, iklo, ikhi) = _qkprep(
        u, w["qk_norm"], tabs, tiles["tt_prep"], n_tblocks)

    y_b = attend_diff(dqb, dkb, dvb)
    y_c = attend_dsa(sqb, skb, svb, iqb, iklo, ikhi, ikw)

    y = jnp.stack([y_a.reshape(m, bw), y_b.reshape(m, bw), y_c.reshape(m, bw), y_d.reshape(m, bw)])
    merged = _merge(h, y, w["w_gate"], w["w_branch"], layer, tiles["tm"], tiles["tn_merge"])
    x = _resid_matmul(x, merged, w["w_out"], layer, tiles["tm"], tiles["tn_out"], "out_proj")

    h2 = _rmsnorm(x, w["norm_ffn"])
    a = _ffn_up(h2, w["w_ffn_gate"], w["w_ffn_up"], layer, tiles["tm_ffn"], tiles["tn_ffn"])
    x = _resid_matmul(x, a, w["w_ffn_down"], layer, tiles["tm_down"], tiles["tn_down"], "ffn_down",
                      resident_rows=True)

    dv = u3[:, :, U_DV * bw:(U_DV + 1) * bw]
    sv = u3[:, :, U_SV * bw:(U_SV + 1) * bw]
    rows = (dkf.reshape(b, t, bw // (2 * HEAD_DIM), 2, HEAD_DIM),
            dv.reshape(b, t, bw // (2 * HEAD_DIM), 2 * HEAD_DIM),
            skf.reshape(b, t, bw // HEAD_DIM, HEAD_DIM),
            sv.reshape(b, t, bw // HEAD_DIM, HEAD_DIM),
            ikw.reshape(b, t, LANES)[:, :, :IDX_DIM])
    return x, sc_new, cf_new, rows


def _tiles(m, t, ffn):
    if m >= 1024:
        tm = _pick_tile(m, (1024, 512, 256))
        return dict(tm=tm, tn_in=1024, tn_merge=512, tn_out=1024,
                    tm_ffn=_pick_tile(m, (2048, 1024, 512, 256)), tn_ffn=_pick_tile(ffn, (256, 128)),
                    tm_down=_pick_tile(m, (1024, 512, 256)), tn_down=256,
                    tt_conv=_pick_tile(t, (128,)), tt_prep=_pick_tile(t, (256, 128)))
    return dict(tm=m, tn_in=1024, tn_merge=512, tn_out=1024, tm_ffn=m, tn_ffn=_pick_tile(ffn, (256, 128)),
                tm_down=m, tn_down=512, tt_conv=t, tt_prep=m)


def kernel(x_prompt, x_sample, cache_diff_k, cache_diff_v, cache_dsa_k, cache_dsa_v, cache_dsa_kidx,
           state_sconv, state_cconv, page_table, norm_mix, w_in, conv_sc, qk_norm, diff_lambda, diff_subln,
           conv_conf, conv_conf_bias, conf_ln, w_gate, w_branch, w_out, norm_ffn, w_ffn_gate, w_ffn_up,
           w_ffn_down):
    bp, tp, d = x_prompt.shape
    bs, ts, _ = x_sample.shape
    depth = w_in.shape[0]
    bw = d // 4
    ffn = w_ffn_gate.shape[-1]
    past = page_table.shape[1] * PAGE_SIZE
    pos_p = jnp.arange(tp, dtype=jnp.int32)
    pos_s = past + jnp.arange(ts, dtype=jnp.int32)
    sc0 = jnp.zeros((bp,) + state_sconv.shape[2:], F32)
    cf0 = jnp.zeros((bp,) + state_cconv.shape[2:], F32)
    tiles_p = _tiles(bp * tp, tp, ffn)
    tiles_s = _tiles(bs * ts, ts, ffn)
    assert bw == GROUPS * HEAD_DIM and cache_diff_k.shape[2] == PAGE_SIZE

    lead = cache_diff_k.shape[:2]
    page_diff_k = cache_diff_k.reshape(lead + (PAGE_ROWS, HEAD_DIM))
    page_diff_v = _value_rows(cache_diff_v, lead)
    page_dsa_k = cache_dsa_k.reshape(lead + (PAGE_ROWS, HEAD_DIM))
    page_dsa_v = cache_dsa_v.reshape(lead + (PAGE_ROWS, HEAD_DIM))
    page_kidx_t = jnp.swapaxes(cache_dsa_kidx, 2, 3)

    w_in_b = _prep_w_in(w_in, bw)
    w_gate_b, w_branch_b, w_out_b, w_ffn_down_b = (a.astype(BF16) for a in (w_gate, w_branch, w_out, w_ffn_down))

    xp = x_prompt.reshape(bp * tp, d)
    xs = x_sample.reshape(bs * ts, d)
    rows_p, rows_s, sc_p, sc_s, cf_p, cf_s = [], [], [], [], [], []
    for l in range(depth):
        lam_init = 0.8 - 0.6 * math.exp(-0.3 * l)
        w = dict(norm_mix=norm_mix[l], w_in=w_in_b, conv_sc=conv_sc[l], qk_norm=qk_norm[l],
                 conv_conf=conv_conf[l], conv_conf_bias=conv_conf_bias[l], conf_ln=conf_ln[l],
                 w_gate=w_gate_b, w_branch=w_branch_b, w_out=w_out_b, norm_ffn=norm_ffn[l],
                 w_ffn_gate=w_ffn_gate, w_ffn_up=w_ffn_up, layer=l, w_ffn_down=w_ffn_down_b)
        dl, sub = diff_lambda[l], diff_subln[l]

        def diff_p(dq, dk, dv):
            return _diff_attn_prompt(dq, dk, dv, dl, sub, bp, tp, lam_init, _pick_tile(tp, (256, 128)))

        def dsa_p(sq, sk, sv, iq, iklo, ikhi, ikw):
            return _dsa_attn_prompt(sq, sk, sv, iq, iklo, ikhi, ikw, bp, tp, _pick_tile(tp, (128,)))

        xp, scn, cfn, rows = _layer(xp, bp, tp, pos_p, sc0, cf0, diff_p, dsa_p, w, tiles_p)
        rows_p.append(rows)
        sc_p.append(scn)
        cf_p.append(cfn)

        def diff_s(dq, dk, dv):
            return _diff_attn_sample(dq, dk, dv, page_diff_k, page_diff_v, l, page_table, dl, sub,
                                     bs, ts, lam_init)

        def dsa_s(sq, sk, sv, iq, iklo, ikhi, ikw):
            return _dsa_attn_sample(sq, sk, sv, iq, iklo, ikw, page_dsa_k, page_dsa_v, page_kidx_t,
                                    l, page_table, bs, ts)

        xs, scn, cfn, rows = _layer(xs, bs, ts, pos_s, state_sconv[l], state_cconv[l], diff_s, dsa_s, w,
                                    tiles_s)
        rows_s.append(rows)
        sc_s.append(scn)
        cf_s.append(cfn)

    def stack(rows, i):
        return jnp.stack([r[i] for r in rows])

    return (xp.reshape(bp, tp, d), xs.reshape(bs, ts, d),
            stack(rows_p, 0), stack(rows_s, 0), stack(rows_p, 1), stack(rows_s, 1),
            stack(rows_p, 2), stack(rows_s, 2), stack(rows_p, 3), stack(rows_s, 3),
            stack(rows_p, 4), stack(rows_s, 4),
            jnp.stack(sc_p), jnp.stack(sc_s), jnp.stack(cf_p), jnp.stack(cf_s))
```

```python
import functools
import math

import jax
import jax.numpy as jnp
from jax import lax
from jax.experimental import pallas as pl
from jax.experimental.pallas import tpu as pltpu

F32 = jnp.float32
BF16 = jnp.bfloat16

HEAD_DIM = 128
IDX_HEADS = 8
IDX_DIM = 64
TOPK_MAX = 256
PAGE_SIZE = 128
ROPE_THETA = 10000.0
NORM_EPS = 1e-6
LANES = 128
SUBLANES = 8
VMEM_LIMIT = 56 * 1024 * 1024
NEG_BIG = -1e30

(U_SC_IN, U_SC_B, U_SC_C, U_DQ, U_DK, U_DV, U_SQ, U_SK, U_SV, U_IDX, U_CF_A, U_CF_B) = range(12)
N_UBLOCKS = 12


def _cparams(n_axes):
    return pltpu.CompilerParams(dimension_semantics=("arbitrary",) * n_axes,
                                vmem_limit_bytes=VMEM_LIMIT)


def _dot(a, b):
    return jnp.dot(a, b, preferred_element_type=F32)


def _dot_nt(a, b):
    return lax.dot_general(a, b, (((1,), (1,)), ((), ())), preferred_element_type=F32)


def _sigmoid(x):
    return 1.0 / (1.0 + jnp.exp(-x))


def _pick_tile(n, candidates):
    for c in candidates:
        if n % c == 0:
            return c
    return n


def _rmsnorm_kernel(x_ref, g_ref, o_ref):
    x = x_ref[...]
    y = x * lax.rsqrt(jnp.mean(x * x, axis=-1, keepdims=True) + NORM_EPS)
    o_ref[...] = (y * g_ref[...]).astype(o_ref.dtype)


def _rmsnorm(x, g):
    m, d = x.shape
    tm = _pick_tile(m, (256, 128, 64, 32, 16, 8))
    return pl.pallas_call(
        _rmsnorm_kernel,
        grid=(m // tm,),
        in_specs=[pl.BlockSpec((tm, d), lambda i: (i, 0)),
                  pl.BlockSpec((1, d), lambda i: (0, 0))],
        out_specs=pl.BlockSpec((tm, d), lambda i: (i, 0)),
        out_shape=jax.ShapeDtypeStruct((m, d), BF16),
        compiler_params=_cparams(1),
        name="rmsnorm",
    )(x, g.reshape(1, d))


def _mm_kernel(x_ref, w_ref, o_ref):
    o_ref[...] = _dot(x_ref[...], w_ref[...]).astype(o_ref.dtype)


def _matmul(x, w, layer, tm, tn, out_dtype):
    m, k = x.shape
    n = w.shape[-1]
    return pl.pallas_call(
        _mm_kernel,
        grid=(m // tm, n // tn),
        in_specs=[pl.BlockSpec((tm, k), lambda i, j: (i, 0)),
                  pl.BlockSpec((None, k, tn), lambda i, j: (layer, 0, j))],
        out_specs=pl.BlockSpec((tm, tn), lambda i, j: (i, j)),
        out_shape=jax.ShapeDtypeStruct((m, n), out_dtype),
        compiler_params=_cparams(2),
        name="in_proj",
    )(x, w)


def _merge_kernel(h_ref, y_ref, wg_ref, wb_ref, o_ref, acc_ref):
    n = pl.program_id(2)
    gate = _sigmoid(_dot(h_ref[...], wg_ref[...]))
    val = gate * _dot(y_ref[...], wb_ref[...])

    @pl.when(n == 0)
    def _():
        acc_ref[...] = val

    @pl.when(n > 0)
    def _():
        acc_ref[...] += val

    @pl.when(n == pl.num_programs(2) - 1)
    def _():
        o_ref[...] = acc_ref[...].astype(o_ref.dtype)


def _merge(h, y, wg, wb, layer, tm, tn):
    m, d = h.shape
    nb, _, bw = y.shape
    return pl.pallas_call(
        _merge_kernel,
        grid=(m // tm, d // tn, nb),
        in_specs=[pl.BlockSpec((tm, d), lambda i, j, n: (i, 0)),
                  pl.BlockSpec((None, tm, bw), lambda i, j, n: (n, i, 0)),
                  pl.BlockSpec((None, None, d, tn), lambda i, j, n: (layer, n, 0, j)),
                  pl.BlockSpec((None, None, bw, tn), lambda i, j, n: (layer, n, 0, j))],
        out_specs=pl.BlockSpec((tm, tn), lambda i, j, n: (i, j)),
        out_shape=jax.ShapeDtypeStruct((m, d), BF16),
        scratch_shapes=[pltpu.VMEM((tm, tn), F32)],
        compiler_params=_cparams(3),
        name="gated_merge",
    )(h, y, wg, wb)


def _resid_mm_kernel(x_ref, a_ref, w_ref, o_ref):
    o_ref[...] = x_ref[...] + _dot(a_ref[...], w_ref[...])


def _resid_matmul(x, a, w, layer, tm, tn, name, resident_rows=False):
    m, k = a.shape
    n = w.shape[-1]
    a_mode = dict(pipeline_mode=pl.Buffered(1)) if resident_rows else {}
    return pl.pallas_call(
        _resid_mm_kernel,
        grid=(m // tm, n // tn),
        in_specs=[pl.BlockSpec((tm, tn), lambda i, j: (i, j)),
                  pl.BlockSpec((tm, k), lambda i, j: (i, 0), **a_mode),
                  pl.BlockSpec((None, k, tn), lambda i, j: (layer, 0, j))],
        out_specs=pl.BlockSpec((tm, tn), lambda i, j: (i, j)),
        out_shape=jax.ShapeDtypeStruct((m, n), F32),
        compiler_params=_cparams(2),
        name=name,
    )(x, a, w)


def _ffn_up_kernel(h_ref, wg_ref, wu_ref, o_ref):
    h = h_ref[...]
    g = _dot(h, wg_ref[...].astype(BF16))
    u = _dot(h, wu_ref[...].astype(BF16))
    o_ref[...] = (g * _sigmoid(g) * u).astype(o_ref.dtype)


def _ffn_up(h, wg, wu, layer, tm, tn):
    m, k = h.shape
    n = wg.shape[-1]
    return pl.pallas_call(
        _ffn_up_kernel,
        grid=(m // tm, n // tn),
        in_specs=[pl.BlockSpec((tm, k), lambda i, j: (i, 0), pipeline_mode=pl.Buffered(1)),
                  pl.BlockSpec((None, k, tn), lambda i, j: (layer, 0, j)),
                  pl.BlockSpec((None, k, tn), lambda i, j: (layer, 0, j))],
        out_specs=pl.BlockSpec((tm, tn), lambda i, j: (i, j)),
        out_shape=jax.ShapeDtypeStruct((m, n), BF16),
        compiler_params=_cparams(2),
        name="ffn_up",
    )(h, wg, wu)


def _sconv_kernel(cin_ref, cb_ref, cc_ref, st_ref, w_ref, y_ref, ns_ref, ext_ref, *, tt, width):
    t = pl.program_id(1)
    hist = width - 1
    base = SUBLANES

    @pl.when(t == 0)
    def _():
        ext_ref[base - hist:base, :] = st_ref[...]

    @pl.when(t > 0)
    def _():
        ext_ref[base - hist:base, :] = ext_ref[base + tt - hist:base + tt, :]

    ext_ref[base:base + tt, :] = cc_ref[...] * cin_ref[...]
    z = None
    for k in range(width):
        term = w_ref[k:k + 1, :] * ext_ref[base - hist + k:base - hist + k + tt, :]
        z = term if z is None else z + term
    y_ref[...] = (cb_ref[...] * z).astype(y_ref.dtype)

    @pl.when(t == pl.num_programs(1) - 1)
    def _():
        ns_ref[...] = ext_ref[base + tt - hist:base + tt, :]


def _sconv(u3, state, w, tt):
    b, t, _ = u3.shape
    width, bw = w.shape
    assert width - 1 <= SUBLANES and (tt >= width - 1 or t == tt)

    def ublock(c):
        return pl.BlockSpec((None, tt, bw), lambda bi, ti, c=c: (bi, ti, c))

    return pl.pallas_call(
        functools.partial(_sconv_kernel, tt=tt, width=width),
        grid=(b, t // tt),
        in_specs=[ublock(U_SC_IN), ublock(U_SC_B), ublock(U_SC_C),
                  pl.BlockSpec((None, width - 1, bw), lambda bi, ti: (bi, 0, 0)),
                  pl.BlockSpec((width, bw), lambda bi, ti: (0, 0))],
        out_specs=[pl.BlockSpec((None, tt, bw), lambda bi, ti: (bi, ti, 0)),
                   pl.BlockSpec((None, width - 1, bw), lambda bi, ti: (bi, 0, 0))],
        out_shape=[jax.ShapeDtypeStruct((b, t, bw), BF16),
                   jax.ShapeDtypeStruct((b, width - 1, bw), F32)],
        scratch_shapes=[pltpu.VMEM((SUBLANES + tt, bw), F32)],
        compiler_params=_cparams(2),
        name="short_conv",
    )(u3, u3, u3, state, w)


def _cconv_kernel(ca_ref, cb_ref, st_ref, w_ref, bias_ref, ln_ref, y_ref, ns_ref, ext_ref, c_ref,
                  *, tt, width, base):
    t = pl.program_id(1)
    hist = width - 1
    bw = ca_ref.shape[-1]

    @pl.when(t == 0)
    def _():
        ext_ref[base - hist:base, :] = st_ref[...]

    @pl.when(t > 0)
    def _():
        ext_ref[base - hist:base, :] = ext_ref[base + tt - hist:base + tt, :]

    ext_ref[base:base + tt, :] = ca_ref[...] * _sigmoid(cb_ref[...])
    for j in range(bw // LANES):
        ls = slice(j * LANES, (j + 1) * LANES)
        acc = None
        for k in range(width):
            term = w_ref[k:k + 1, ls] * ext_ref[base - hist + k:base - hist + k + tt, ls]
            acc = term if acc is None else acc + term
        c_ref[:, ls] = acc + bias_ref[:, ls]
    c = c_ref[...]
    xc = c - jnp.mean(c, axis=-1, keepdims=True)
    yn = xc * lax.rsqrt(jnp.mean(xc * xc, axis=-1, keepdims=True) + NORM_EPS)
    yn = yn * ln_ref[0:1, :] + ln_ref[1:2, :]
    y_ref[...] = (yn * _sigmoid(yn)).astype(y_ref.dtype)

    @pl.when(t == pl.num_programs(1) - 1)
    def _():
        ns_ref[...] = ext_ref[base + tt - hist:base + tt, :]


def _cconv(u3, state, w, bias, ln, tt):
    b, t, _ = u3.shape
    width, bw = w.shape
    hist = width - 1
    base = -(-hist // SUBLANES) * SUBLANES
    assert tt >= hist or t == tt

    def ublock(c):
        return pl.BlockSpec((None, tt, bw), lambda bi, ti, c=c: (bi, ti, c))

    return pl.pallas_call(
        functools.partial(_cconv_kernel, tt=tt, width=width, base=base),
        grid=(b, t // tt),
        in_specs=[ublock(U_CF_A), ublock(U_CF_B),
                  pl.BlockSpec((None, hist, bw), lambda bi, ti: (bi, 0, 0)),
                  pl.BlockSpec((width, bw), lambda bi, ti: (0, 0)),
                  pl.BlockSpec((1, bw), lambda bi, ti: (0, 0)),
                  pl.BlockSpec((2, bw), lambda bi, ti: (0, 0))],
        out_specs=[pl.BlockSpec((None, tt, bw), lambda bi, ti: (bi, ti, 0)),
                   pl.BlockSpec((None, hist, bw), lambda bi, ti: (bi, 0, 0))],
        out_shape=[jax.ShapeDtypeStruct((b, t, bw), BF16),
                   jax.ShapeDtypeStruct((b, hist, bw), F32)],
        scratch_shapes=[pltpu.VMEM((base + tt, bw), F32), pltpu.VMEM((tt, bw), F32)],
        compiler_params=_cparams(2),
        name="conformer_conv",
    )(u3, u3, state, w, bias.reshape(1, bw), ln)


def _qkprep_kernel(dq_ref, dk_ref, dv_ref, sq_ref, sk_ref, sv_ref, idx_ref, g_ref,
                   cos_ref, sin_ref, cos64_ref, sina_ref, sinb_ref,
                   dqb_ref, dkf_ref, dkb_ref, dvb_ref, sqb_ref, skf_ref, skb_ref, svb_ref,
                   iqb_ref, ikw_ref, iklo_ref, ikhi_ref):
    cos = cos_ref[...]
    sin = sin_ref[...]
    n_heads = dq_ref.shape[-1] // HEAD_DIM

    def norm_rope(x, g):
        y = x * lax.rsqrt(jnp.mean(x * x, axis=-1, keepdims=True) + NORM_EPS) * g
        return y * cos + pltpu.roll(y, HEAD_DIM // 2, 1) * sin

    for h in range(n_heads):
        ls = slice(h * HEAD_DIM, (h + 1) * HEAD_DIM)
        dqb_ref[:, ls] = norm_rope(dq_ref[:, ls], g_ref[0:1, :]).astype(BF16)
        dk = norm_rope(dk_ref[:, ls], g_ref[1:2, :])
        dkf_ref[:, ls] = dk
        dkb_ref[:, ls] = dk.astype(BF16)
        sqb_ref[:, ls] = norm_rope(sq_ref[:, ls], g_ref[2:3, :]).astype(BF16)
        sk = norm_rope(sk_ref[:, ls], g_ref[3:4, :])
        skf_ref[:, ls] = sk
        skb_ref[:, ls] = sk.astype(BF16)
    dvb_ref[...] = dv_ref[...].astype(BF16)
    svb_ref[...] = sv_ref[...].astype(BF16)

    cos64 = cos64_ref[...]
    sina = sina_ref[...]
    sinb = sinb_ref[...]

    def rope64(x):
        return (x * cos64 + pltpu.roll(x, LANES - IDX_DIM // 2, 1) * sina
                + pltpu.roll(x, IDX_DIM // 2, 1) * sinb)

    n_iq = IDX_HEADS * IDX_DIM
    for j in range(n_iq // LANES):
        ls = slice(j * LANES, (j + 1) * LANES)
        iqb_ref[:, ls] = rope64(idx_ref[:, ls]).astype(BF16)
    xs = idx_ref[:, n_iq:n_iq + LANES]
    lane = lax.broadcasted_iota(jnp.int32, xs.shape, 1)
    ik = jnp.where(lane < IDX_DIM, rope64(xs), 0.0)
    iw = jnp.where((lane >= IDX_DIM) & (lane < IDX_DIM + IDX_HEADS), xs * (IDX_HEADS ** -0.5), 0.0)
    ikw_ref[...] = ik + iw
    iklo_ref[...] = ik.astype(BF16)
    ikhi_ref[...] = pltpu.roll(ik, IDX_DIM, 1).astype(BF16)


def _qkprep(u, qk_norm, tabs, tt, n_tblocks):
    m = u.shape[0]
    bw = u.shape[1] // N_UBLOCKS

    def ublock(c):
        return pl.BlockSpec((tt, bw), lambda i, c=c: (i, c))

    def tab():
        return pl.BlockSpec((tt, LANES), lambda i: (i % n_tblocks, 0))

    def out(w):
        return pl.BlockSpec((tt, w), lambda i: (i, 0))

    n_iq = IDX_HEADS * IDX_DIM
    shapes = [(bw, BF16), (bw, F32), (bw, BF16), (bw, BF16), (bw, BF16), (bw, F32), (bw, BF16), (bw, BF16),
              (n_iq, BF16), (LANES, F32), (LANES, BF16), (LANES, BF16)]
    return pl.pallas_call(
        _qkprep_kernel,
        grid=(m // tt,),
        in_specs=[ublock(U_DQ), ublock(U_DK), ublock(U_DV), ublock(U_SQ), ublock(U_SK), ublock(U_SV),
                  ublock(U_IDX), pl.BlockSpec((4, HEAD_DIM), lambda i: (0, 0)),
                  tab(), tab(), tab(), tab(), tab()],
        out_specs=[out(w) for w, _ in shapes],
        out_shape=[jax.ShapeDtypeStruct((m, w), dt) for w, dt in shapes],
        compiler_params=_cparams(1),
        name="qk_norm_rope",
    )(u, u, u, u, u, u, u, qk_norm, *tabs)


def _rope_tables(pos):
    def angles(d):
        inv = ROPE_THETA ** (-jnp.arange(0, d, 2, dtype=F32) / d)
        return pos.astype(F32)[:, None] * inv[None, :]

    a = angles(HEAD_DIM)
    cos = jnp.concatenate([jnp.cos(a), jnp.cos(a)], axis=1)
    sin = jnp.concatenate([-jnp.sin(a), jnp.sin(a)], axis=1)
    a2 = angles(IDX_DIM)
    c2, s2, z2 = jnp.cos(a2), jnp.sin(a2), jnp.zeros_like(a2)
    cos64 = jnp.concatenate([c2, c2, c2, c2], axis=1)
    sina = jnp.concatenate([-s2, z2, -s2, z2], axis=1)
    sinb = jnp.concatenate([z2, s2, z2, s2], axis=1)
    return (cos, sin, cos64, sina, sinb)


def _diff_lambda(dl_ref, lam_init):
    lv = dl_ref[...]
    a = jnp.sum(lv[0:1, :] * lv[1:2, :], axis=-1, keepdims=True)
    b = jnp.sum(lv[2:3, :] * lv[3:4, :], axis=-1, keepdims=True)
    return jnp.exp(a) - jnp.exp(b) + lam_init


def _subln(o, g, lam_init):
    y = o * lax.rsqrt(jnp.mean(o * o, axis=-1, keepdims=True) + NORM_EPS)
    return (y * g) * (1.0 - lam_init)


def _topk_select(score, col, k, idx_bits):
    r = score.shape[0]
    bits = pltpu.bitcast(score + 0.0, jnp.int32)
    key = jnp.where(bits < 0, bits ^ jnp.int32(0x7FFFFFFF), bits)
    kf = float(k)

    def count(pred):
        return jnp.sum(jnp.where(pred, 1.0, 0.0), axis=-1, keepdims=True)

    zero = jnp.zeros((r, 1), jnp.int32)
    start = jnp.where(count(key >= zero) >= kf, zero, jnp.full((r, 1), -2 ** 31, jnp.int32))

    def value_step(i, cur):
        cand = cur + jnp.left_shift(jnp.int32(1), 30 - i)
        return jnp.where(count(key >= cand) >= kf, cand, cur)

    thr = lax.fori_loop(0, 31, value_step, start)
    above = key > thr
    tie = key == thr
    need = kf - count(above)

    def index_step(i, cur):
        cand = cur + jnp.left_shift(jnp.int32(1), idx_bits - 1 - i)
        cnt = jnp.sum(jnp.where(tie, jnp.where(col < cand, 1.0, 0.0), 0.0), axis=-1, keepdims=True)
        return jnp.where(cnt <= need, cand, cur)

    bound = lax.fori_loop(0, idx_bits, index_step, zero)
    return above | (tie & (col < bound))


def _diff_attn_kernel(q_ref, k_ref, v_ref, dl_ref, g_ref, o_ref, *, tq, q0, lam_init):
    qi = pl.program_id(2)
    q = q_ref[...]
    k = k_ref[...]
    t = k.shape[0]
    row = q0 + qi * tq + lax.broadcasted_iota(jnp.int32, (tq, t), 0)
    col = lax.broadcasted_iota(jnp.int32, (tq, t), 1)
    mask = col <= row
    scale = HEAD_DIM ** -0.5

    def softmax(c):
        ls = slice(c * HEAD_DIM, (c + 1) * HEAD_DIM)
        s = jnp.where(mask, _dot_nt(q[:, ls], k[:, ls]) * scale, -jnp.inf)
        e = jnp.exp(s - jnp.max(s, axis=-1, keepdims=True))
        return e * (1.0 / jnp.sum(e, axis=-1, keepdims=True))

    lam = _diff_lambda(dl_ref, lam_init)
    p = softmax(0) - lam * softmax(1)
    o = _dot(p.astype(BF16), v_ref[...])
    o_ref[...] = _subln(o, g_ref[...], lam_init).astype(o_ref.dtype)


def _causal_groups(t, tq):
    n = _pick_tile(t // tq, (8, 4, 2, 1))
    tg = t // n
    return [(g * tg, tg) for g in range(n)]


def _diff_attn_prompt(dq, dk, dv, diff_lambda, subln, b, t, lam_init, tq):
    bw = dq.shape[-1]
    vd = 2 * HEAD_DIM
    nh = bw // vd
    q3, k3, v3 = (a.reshape(b, t, bw) for a in (dq, dk, dv))
    outs = []
    for q0, tg in _causal_groups(t, tq):
        tk = q0 + tg
        qb0 = q0 // tq
        outs.append(pl.pallas_call(
            functools.partial(_diff_attn_kernel, tq=tq, q0=q0, lam_init=lam_init),
            grid=(b, nh, tg // tq),
            in_specs=[pl.BlockSpec((None, tq, vd), lambda bi, h, qi, qb0=qb0: (bi, qb0 + qi, h)),
                      pl.BlockSpec((None, tk, vd), lambda bi, h, qi: (bi, 0, h)),
                      pl.BlockSpec((None, tk, vd), lambda bi, h, qi: (bi, 0, h)),
                      pl.BlockSpec((4, HEAD_DIM), lambda bi, h, qi: (0, 0)),
                      pl.BlockSpec((1, vd), lambda bi, h, qi: (0, 0))],
            out_specs=pl.BlockSpec((None, tq, vd), lambda bi, h, qi: (bi, qi, h)),
            out_shape=jax.ShapeDtypeStruct((b, tg, bw), BF16),
            compiler_params=_cparams(3),
            name="diff_attn_prompt",
        )(q3, k3, v3, diff_lambda, subln.reshape(1, vd)))
    return jnp.concatenate(outs, axis=1)


def _dsa_attn_kernel(q_ref, k_ref, v_ref, iq_ref, iklo_ref, ikhi_ref, iw_ref, o_ref, *, tq, q0, topk,
                     idx_bits):
    qi = pl.program_id(1)
    t = k_ref.shape[0]
    iklo = iklo_ref[...]
    ikhi = ikhi_ref[...]
    iw = iw_ref[...]
    score = None
    for j in range(IDX_HEADS // 2):
        qpair = iq_ref[:, j * LANES:(j + 1) * LANES]
        for half, ik in enumerate((iklo, ikhi)):
            h = 2 * j + half
            sc = jnp.maximum(_dot_nt(qpair, ik) * (IDX_DIM ** -0.5), 0.0)
            term = iw[:, IDX_DIM + h:IDX_DIM + h + 1] * sc
            score = term if score is None else score + term
    row = q0 + qi * tq + lax.broadcasted_iota(jnp.int32, (tq, t), 0)
    col = lax.broadcasted_iota(jnp.int32, (tq, t), 1)
    causal = col <= row
    sel = _topk_select(jnp.where(causal, score, -jnp.inf), col, topk, idx_bits)
    keep = causal & sel
    scale = HEAD_DIM ** -0.5
    for h in range(q_ref.shape[-1] // HEAD_DIM):
        ls = slice(h * HEAD_DIM, (h + 1) * HEAD_DIM)
        s = jnp.where(keep, _dot_nt(q_ref[:, ls], k_ref[:, ls]) * scale, -jnp.inf)
        e = jnp.exp(s - jnp.max(s, axis=-1, keepdims=True))
        a = e * (1.0 / jnp.sum(e, axis=-1, keepdims=True))
        o_ref[:, ls] = _dot(a.astype(BF16), v_ref[:, ls]).astype(o_ref.dtype)


def _dsa_attn_prompt(sq, sk, sv, iq, iklo, ikhi, ikw, b, t, tq):
    bw = sq.shape[-1]
    n_iq = iq.shape[-1]
    topk = min(TOPK_MAX, t // 4)
    args = (sq.reshape(b, t, bw), sk.reshape(b, t, bw), sv.reshape(b, t, bw), iq.reshape(b, t, n_iq),
            iklo.reshape(b, t, LANES), ikhi.reshape(b, t, LANES), ikw.reshape(b, t, LANES))
    outs = []
    for q0, tg in _causal_groups(t, tq):
        tk = q0 + tg
        qb0 = q0 // tq

        def qblock(w, qb0=qb0):
            return pl.BlockSpec((None, tq, w), lambda bi, qi: (bi, qb0 + qi, 0))

        def kblock(w, tk=tk):
            return pl.BlockSpec((None, tk, w), lambda bi, qi: (bi, 0, 0))

        outs.append(pl.pallas_call(
            functools.partial(_dsa_attn_kernel, tq=tq, q0=q0, topk=topk, idx_bits=int(tk).bit_length()),
            grid=(b, tg // tq),
            in_specs=[qblock(bw), kblock(bw), kblock(bw), qblock(n_iq), kblock(LANES), kblock(LANES),
                      qblock(LANES)],
            out_specs=pl.BlockSpec((None, tq, bw), lambda bi, qi: (bi, qi, 0)),
            out_shape=jax.ShapeDtypeStruct((b, tg, bw), BF16),
            compiler_params=_cparams(2),
            name="dsa_attn_prompt",
        )(*args))
    return jnp.concatenate(outs, axis=1)


GROUPS = SUBLANES
PAGE_ROWS = PAGE_SIZE * GROUPS


def _pages_per_step(n_pages):
    return _pick_tile(n_pages, (8, 4, 2, 1))


def _page_specs(layer, npp, n_steps, rows, width):
    def spec(r):
        def index(bi, p, pt):
            return (layer, pt[bi, jnp.minimum(p, n_steps - 1) * npp + r], 0, 0)
        return pl.BlockSpec((None, None, rows, width), index)
    return [spec(r) for r in range(npp)]


def _group_match(shape):
    lane = lax.broadcasted_iota(jnp.int32, shape, 1)
    row = lax.broadcasted_iota(jnp.int32, shape, 0)
    return (lane & (GROUPS - 1)) == (row >> 3)


def _softmax_update(s, keep, m_ref, l_ref):
    s = jnp.where(keep, s, NEG_BIG)
    m_old = m_ref[...]
    m_new = jnp.maximum(m_old, jnp.max(s, axis=-1, keepdims=True))
    alpha = jnp.exp(m_old - m_new)
    p = jnp.where(keep, jnp.exp(s - m_new), 0.0)
    l_ref[...] = alpha * l_ref[...] + jnp.sum(p, axis=-1, keepdims=True)
    m_ref[...] = m_new
    return p, alpha


def _softmax_update_biased(s, m_ref, l_ref):
    m_old = m_ref[...]
    m_new = jnp.maximum(m_old, jnp.max(s, axis=-1, keepdims=True))
    alpha = jnp.exp(m_old - m_new)
    p = jnp.exp(s - m_new)
    l_ref[...] = alpha * l_ref[...] + jnp.sum(p, axis=-1, keepdims=True)
    m_ref[...] = m_new
    return p, alpha


def _group_bias(shape):
    return jnp.where(_group_match(shape), 0.0, NEG_BIG)


def _init_softmax_state(m_ref, l_ref, acc_ref):
    m_ref[...] = jnp.full(m_ref.shape, NEG_BIG, F32)
    l_ref[...] = jnp.zeros(l_ref.shape, F32)
    acc_ref[...] = jnp.zeros(acc_ref.shape, F32)


def _new_key_mask(shape, t_new):
    key = lax.broadcasted_iota(jnp.int32, shape, 1) >> 3
    qrow = lax.broadcasted_iota(jnp.int32, shape, 0) & (SUBLANES - 1)
    return _group_match(shape) & (key <= qrow) & (key < t_new)


def _value_row_shift(h, c, half, n_heads):
    return (half * n_heads + h) - (2 * h + c)


def _spread_weights(p, n_heads):
    n = p.shape[1]
    tiles = []
    for h in range(n_heads):
        for c in range(2):
            tile = p[(2 * h + c) * SUBLANES:(2 * h + c + 1) * SUBLANES, :]
            for half in range(2):
                shift = _value_row_shift(h, c, half, n_heads) % n
                tiles.append(tile if shift == 0 else pltpu.roll(tile, shift, 1))
    return jnp.concatenate(tiles, axis=0)


def _spread_rows(a):
    tiles = []
    for g in range(GROUPS):
        tile = a[g * SUBLANES:(g + 1) * SUBLANES, :]
        tiles += [tile, tile]
    return jnp.concatenate(tiles, axis=0)


def _diff_attn_sample_kernel(pt_ref, q_ref, *refs, npp, n_steps, t_new, lam_init):
    k_refs = refs[:npp]
    v_refs = refs[npp:2 * npp]
    kn_ref, vn_ref, dl_ref, g_ref, o_ref, m_ref, l_ref, acc_ref, bias_ref = refs[2 * npp:]
    p = pl.program_id(1)
    n_heads = GROUPS // 2
    scale = HEAD_DIM ** -0.5

    @pl.when(p == 0)
    def _():
        _init_softmax_state(m_ref, l_ref, acc_ref)
        bias_ref[...] = _group_bias(bias_ref.shape)

    def accumulate(w, alpha, v_parts):
        pv = None
        for r, v in enumerate(v_parts):
            rows = v.shape[0]
            part = _dot(_spread_weights(w[:, r * rows:(r + 1) * rows], n_heads).astype(BF16), v)
            pv = part if pv is None else pv + part
        acc_ref[...] = _spread_rows(alpha) * acc_ref[...] + pv

    @pl.when(p < n_steps)
    def _():
        q = q_ref[...]
        s = jnp.concatenate([_dot_nt(q, kr[...].astype(BF16)) for kr in k_refs], axis=1) * scale
        w, alpha = _softmax_update_biased(s + bias_ref[...], m_ref, l_ref)
        accumulate(w, alpha, [vr[...].astype(BF16) for vr in v_refs])

    @pl.when(p == n_steps)
    def _():
        s = _dot_nt(q_ref[...], kn_ref[...]) * scale
        w, alpha = _softmax_update(s, _new_key_mask(s.shape, t_new), m_ref, l_ref)
        accumulate(w, alpha, [vn_ref[...]])
        out = acc_ref[...] * _spread_rows(1.0 / l_ref[...])
        lam = _diff_lambda(dl_ref, lam_init)
        vd = 2 * HEAD_DIM
        for h in range(n_heads):
            def tile(c, half):
                r0 = ((2 * h + c) * 2 + half) * SUBLANES
                return out[r0:r0 + SUBLANES, :]
            o = jnp.concatenate([tile(0, half) - lam * tile(1, half) for half in range(2)], axis=1)
            o_ref[:, h * vd:(h + 1) * vd] = _subln(o, g_ref[...], lam_init).astype(o_ref.dtype)


def _group_major_queries(q, b, t):
    qg = q.reshape(b, t, GROUPS, HEAD_DIM).transpose(0, 2, 1, 3)
    qg = jnp.pad(qg, ((0, 0), (0, 0), (0, SUBLANES - t), (0, 0)))
    return qg.reshape(b, GROUPS * SUBLANES, HEAD_DIM)


def _pad_rows(a, rows):
    return jnp.pad(a, ((0, 0), (0, rows - a.shape[1]), (0, 0)))


def _value_rows(v, lead):
    n_heads = GROUPS // 2
    v = v.reshape(lead + (-1, n_heads, 2, HEAD_DIM))
    v = jnp.swapaxes(v, -3, -2)
    return v.reshape(lead + (-1, HEAD_DIM))


def _diff_attn_sample(dq, dk, dv, cache_k, cache_v, layer, page_table, diff_lambda, subln, b, t, lam_init):
    bw = dq.shape[-1]
    vd = 2 * HEAD_DIM
    n_pages = page_table.shape[1]
    npp = _pages_per_step(n_pages)
    n_steps = n_pages // npp
    rows = GROUPS * SUBLANES
    new_rows = LANES
    qg = _group_major_queries(dq, b, t)
    kn = _pad_rows(dk.reshape(b, t * GROUPS, HEAD_DIM), new_rows)
    vn = _pad_rows(_value_rows(dv.reshape(b, t, GROUPS // 2, vd), (b,)), new_rows)

    def per_batch(r, w):
        return pl.BlockSpec((None, r, w), lambda bi, p, pt: (bi, 0, 0))

    grid_spec = pltpu.PrefetchScalarGridSpec(
        num_scalar_prefetch=1,
        grid=(b, n_steps + 1),
        in_specs=[per_batch(rows, HEAD_DIM)]
        + _page_specs(layer, npp, n_steps, PAGE_ROWS, HEAD_DIM)
        + _page_specs(layer, npp, n_steps, PAGE_ROWS, HEAD_DIM)
        + [per_batch(new_rows, HEAD_DIM), per_batch(new_rows, HEAD_DIM),
           pl.BlockSpec((4, HEAD_DIM), lambda bi, p, pt: (0, 0)),
           pl.BlockSpec((1, vd), lambda bi, p, pt: (0, 0))],
        out_specs=per_batch(SUBLANES, bw),
        scratch_shapes=[pltpu.VMEM((rows, 1), F32), pltpu.VMEM((rows, 1), F32),
                        pltpu.VMEM((2 * rows, HEAD_DIM), F32), pltpu.VMEM((rows, npp * PAGE_ROWS), F32)],
    )
    out = pl.pallas_call(
        functools.partial(_diff_attn_sample_kernel, npp=npp, n_steps=n_steps, t_new=t, lam_init=lam_init),
        grid_spec=grid_spec,
        out_shape=jax.ShapeDtypeStruct((b, SUBLANES, bw), BF16),
        compiler_params=_cparams(2),
        name="diff_attn_sample",
    )(page_table, qg, *([cache_k] * npp), *([cache_v] * npp), kn, vn, diff_lambda, subln.reshape(1, vd))
    return out[:, :t].reshape(b * t, bw)


def _idx_score_kernel(pt_ref, q_ref, w_ref, *refs, npp, n_steps):
    k_refs = refs[:npp]
    kn_ref, o_ref = refs[npp:]
    p = pl.program_id(1)
    q = q_ref[...]
    w = w_ref[...]

    def scores(kt):
        sc = jnp.maximum(_dot(q, kt) * (IDX_DIM ** -0.5), 0.0) * w
        out = sc[0:SUBLANES]
        for h in range(1, IDX_HEADS):
            out = out + sc[h * SUBLANES:(h + 1) * SUBLANES]
        return out

    @pl.when(p < n_steps)
    def _():
        for r, kr in enumerate(k_refs):
            o_ref[:, r * PAGE_SIZE:(r + 1) * PAGE_SIZE] = scores(kr[...].astype(BF16))

    @pl.when(p == n_steps)
    def _():
        o_ref[...] = scores(kn_ref[...])


def _select_sample_kernel(s_ref, o_ref, *, past, t_new, topk, idx_bits):
    s = s_ref[...]
    qrow = jnp.minimum(lax.broadcasted_iota(jnp.int32, s.shape, 0), t_new - 1)
    col = lax.broadcasted_iota(jnp.int32, s.shape, 1)
    admissible = (col <= past + qrow) & (col < past + t_new)
    sel = _topk_select(jnp.where(admissible, s, -jnp.inf), col, topk, idx_bits)
    o_ref[...] = jnp.where(admissible, jnp.where(sel, 1.0, 0.0), 0.0)


def _dsa_attn_sample_kernel(pt_ref, q_ref, keep_ref, spread_ref, *refs, npp, n_steps):
    k_refs = refs[:npp]
    v_refs = refs[npp:2 * npp]
    kn_ref, vn_ref, o_ref, m_ref, l_ref, acc_ref, bias_ref = refs[2 * npp:]
    p = pl.program_id(1)
    scale = HEAD_DIM ** -0.5

    @pl.when(p == 0)
    def _():
        _init_softmax_state(m_ref, l_ref, acc_ref)
        bias_ref[...] = _group_bias(bias_ref.shape)

    def keep_lanes(n_pages):
        spread = spread_ref[...]
        parts = [_dot(keep_ref[:, r * PAGE_SIZE:(r + 1) * PAGE_SIZE].astype(BF16), spread)
                 for r in range(n_pages)]
        return parts[0] if n_pages == 1 else jnp.concatenate(parts, axis=1)

    def keep_mask(n):
        kp = keep_lanes(1)[:, 0:n] > 0.5
        return _group_match((GROUPS * SUBLANES, n)) & jnp.concatenate([kp] * GROUPS, axis=0)

    def accumulate(w, alpha, v_parts):
        pv = None
        for r, v in enumerate(v_parts):
            rows = v.shape[0]
            part = _dot(w[:, r * rows:(r + 1) * rows].astype(BF16), v)
            pv = part if pv is None else pv + part
        acc_ref[...] = alpha * acc_ref[...] + pv

    @pl.when(p < n_steps)
    def _():
        q = q_ref[...]
        s = jnp.concatenate([_dot_nt(q, kr[...].astype(BF16)) for kr in k_refs], axis=1) * scale
        keep_bias = (keep_lanes(npp) - 1.0) * (-NEG_BIG)
        s = s + bias_ref[...] + jnp.concatenate([keep_bias] * GROUPS, axis=0)
        w, alpha = _softmax_update_biased(s, m_ref, l_ref)
        accumulate(w, alpha, [vr[...].astype(BF16) for vr in v_refs])

    @pl.when(p == n_steps)
    def _():
        s = _dot_nt(q_ref[...], kn_ref[...]) * scale
        w, alpha = _softmax_update(s, keep_mask(s.shape[1]), m_ref, l_ref)
        accumulate(w, alpha, [vn_ref[...]])
        o_ref[...] = (acc_ref[...] * (1.0 / l_ref[...])).astype(o_ref.dtype)


def _dsa_attn_sample(sq, sk, sv, iq, ikb, ikw, cache_k, cache_v, cache_kidx, layer, page_table, b, t):
    bw = sq.shape[-1]
    n_pages = page_table.shape[1]
    past = n_pages * PAGE_SIZE
    npp = _pages_per_step(n_pages)
    n_steps = n_pages // npp
    step_w = npp * PAGE_SIZE
    s_pad = (n_steps + 1) * step_w
    topk = min(TOPK_MAX, (past + t) // 4)
    idx_bits = int(s_pad).bit_length()

    def per_batch(r, w):
        return pl.BlockSpec((None, r, w), lambda bi, p, pt: (bi, 0, 0))

    n_iq = iq.shape[-1]
    qi = iq.reshape(b, t, IDX_HEADS, IDX_DIM).transpose(0, 2, 1, 3)
    qi = jnp.pad(qi, ((0, 0), (0, 0), (0, SUBLANES - t), (0, 0))).reshape(b, IDX_HEADS * SUBLANES, IDX_DIM)
    wi = ikw.reshape(b, t, LANES)[:, :, IDX_DIM:IDX_DIM + IDX_HEADS].transpose(0, 2, 1)
    wi = jnp.pad(wi, ((0, 0), (0, 0), (0, SUBLANES - t))).reshape(b, IDX_HEADS * SUBLANES, 1)
    kin = jnp.swapaxes(_pad_rows(ikb.reshape(b, t, LANES)[:, :, :IDX_DIM], step_w), 1, 2)
    scores = pl.pallas_call(
        functools.partial(_idx_score_kernel, npp=npp, n_steps=n_steps),
        grid_spec=pltpu.PrefetchScalarGridSpec(
            num_scalar_prefetch=1,
            grid=(b, n_steps + 1),
            in_specs=[per_batch(IDX_HEADS * SUBLANES, IDX_DIM), per_batch(IDX_HEADS * SUBLANES, 1)]
            + _page_specs(layer, npp, n_steps, IDX_DIM, PAGE_SIZE) + [per_batch(IDX_DIM, step_w)],
            out_specs=pl.BlockSpec((None, SUBLANES, step_w), lambda bi, p, pt: (bi, 0, p)),
        ),
        out_shape=jax.ShapeDtypeStruct((b, SUBLANES, s_pad), F32),
        compiler_params=_cparams(2),
        name="idx_score_sample",
    )(page_table, qi, wi, *([cache_kidx] * npp), kin)

    keep = pl.pallas_call(
        functools.partial(_select_sample_kernel, past=past, t_new=t, topk=topk, idx_bits=idx_bits),
        grid=(b,),
        in_specs=[pl.BlockSpec((None, SUBLANES, s_pad), lambda bi: (bi, 0, 0))],
        out_specs=pl.BlockSpec((None, SUBLANES, s_pad), lambda bi: (bi, 0, 0)),
        out_shape=jax.ShapeDtypeStruct((b, SUBLANES, s_pad), F32),
        compiler_params=_cparams(1),
        name="topk_select_sample",
    )(scores)

    rows = GROUPS * SUBLANES
    new_rows = LANES
    spread = jnp.repeat(jnp.eye(PAGE_SIZE, dtype=BF16), GROUPS, axis=1)
    qg = _group_major_queries(sq, b, t)
    kn = _pad_rows(sk.reshape(b, t * GROUPS, HEAD_DIM), new_rows)
    vn = _pad_rows(sv.reshape(b, t * GROUPS, HEAD_DIM), new_rows)
    out = pl.pallas_call(
        functools.partial(_dsa_attn_sample_kernel, npp=npp, n_steps=n_steps),
        grid_spec=pltpu.PrefetchScalarGridSpec(
            num_scalar_prefetch=1,
            grid=(b, n_steps + 1),
            in_specs=[per_batch(rows, HEAD_DIM),
                      pl.BlockSpec((None, SUBLANES, step_w), lambda bi, p, pt: (bi, 0, p)),
                      pl.BlockSpec((PAGE_SIZE, PAGE_ROWS), lambda bi, p, pt: (0, 0))]
            + _page_specs(layer, npp, n_steps, PAGE_ROWS, HEAD_DIM)
            + _page_specs(layer, npp, n_steps, PAGE_ROWS, HEAD_DIM)
            + [per_batch(new_rows, HEAD_DIM), per_batch(new_rows, HEAD_DIM)],
            out_specs=per_batch(rows, HEAD_DIM),
            scratch_shapes=[pltpu.VMEM((rows, 1), F32), pltpu.VMEM((rows, 1), F32),
                            pltpu.VMEM((rows, HEAD_DIM), F32), pltpu.VMEM((rows, npp * PAGE_ROWS), F32)],
        ),
        out_shape=jax.ShapeDtypeStruct((b, rows, HEAD_DIM), BF16),
        compiler_params=_cparams(2),
        name="dsa_attn_sample",
    )(page_table, qg, keep, spread, *([cache_k] * npp), *([cache_v] * npp), kn, vn)
    out = out.reshape(b, GROUPS, SUBLANES, HEAD_DIM)[:, :, :t].transpose(0, 2, 1, 3)
    return out.reshape(b * t, bw)


def _prep_w_in(w, bw):
    n_idx = IDX_HEADS * IDX_DIM + IDX_DIM + IDX_HEADS
    wb = w.astype(BF16)
    main = wb[..., :9 * bw]
    idx = jnp.pad(wb[..., 9 * bw:9 * bw + n_idx], ((0, 0), (0, 0), (0, bw - n_idx)))
    return jnp.concatenate([main, idx, wb[..., 9 * bw + n_idx:]], axis=-1)


def _layer(x, b, t, pos, sc_state, cf_state, attend_diff, attend_dsa, w, tiles):
    m, d = x.shape
    bw = d // 4
    h = _rmsnorm(x, w["norm_mix"])
    layer = w["layer"]
    u = _matmul(h, w["w_in"], layer, tiles["tm"], tiles["tn_in"], F32)
    u3 = u.reshape(b, t, N_UBLOCKS * bw)

    y_a, sc_new = _sconv(u3, sc_state, w["conv_sc"], tiles["tt_conv"])
    y_d, cf_new = _cconv(u3, cf_state, w["conv_conf"], w["conv_conf_bias"], w["conf_ln"], tiles["tt_conv"])

    tabs = _rope_tables(pos)
    if tiles["tt_prep"] > t:
        tabs = tuple(jnp.tile(a, (tiles["tt_prep"] // t, 1)) for a in tabs)
    n_tblocks = max(1, t // tiles["tt_prep"])
    (dqb, dkf, dkb, dvb, sqb, skf, skb, svb, iqb, ikw, iklo, ikhi) = _qkprep(
        u, w["qk_norm"], tabs, tiles["tt_prep"], n_tblocks)

    y_b = attend_diff(dqb, dkb, dvb)
    y_c = attend_dsa(sqb, skb, svb, iqb, iklo, ikhi, ikw)

    y = jnp.stack([y_a.reshape(m, bw), y_b.reshape(m, bw), y_c.reshape(m, bw), y_d.reshape(m, bw)])
    merged = _merge(h, y, w["w_gate"], w["w_branch"], layer, tiles["tm"], tiles["tn_merge"])
    x = _resid_matmul(x, merged, w["w_out"], layer, tiles["tm"], tiles["tn_out"], "out_proj")

    h2 = _rmsnorm(x, w["norm_ffn"])
    a = _ffn_up(h2, w["w_ffn_gate"], w["w_ffn_up"], layer, tiles["tm_ffn"], tiles["tn_ffn"])
    x = _resid_matmul(x, a, w["w_ffn_down"], layer, tiles["tm_down"], tiles["tn_down"], "ffn_down",
                      resident_rows=True)

    dv = u3[:, :, U_DV * bw:(U_DV + 1) * bw]
    sv = u3[:, :, U_SV * bw:(U_SV + 1) * bw]
    rows = (dkf.reshape(b, t, bw // (2 * HEAD_DIM), 2, HEAD_DIM),
            dv.reshape(b, t, bw // (2 * HEAD_DIM), 2 * HEAD_DIM),
            skf.reshape(b, t, bw // HEAD_DIM, HEAD_DIM),
            sv.reshape(b, t, bw // HEAD_DIM, HEAD_DIM),
            ikw.reshape(b, t, LANES)[:, :, :IDX_DIM])
    return x, sc_new, cf_new, rows


def _tiles(m, t, ffn):
    if m >= 1024:
        tm = _pick_tile(m, (1024, 512, 256))
        return dict(tm=tm, tn_in=1024, tn_merge=512, tn_out=1024,
                    tm_ffn=_pick_tile(m, (2048, 1024, 512, 256)), tn_ffn=_pick_tile(ffn, (256, 128)),
                    tm_down=_pick_tile(m, (1024, 512, 256)), tn_down=256,
                    tt_conv=_pick_tile(t, (128,)), tt_prep=_pick_tile(t, (256, 128)))
    return dict(tm=m, tn_in=1024, tn_merge=512, tn_out=1024, tm_ffn=m, tn_ffn=_pick_tile(ffn, (256, 128)),
                tm_down=m, tn_down=512, tt_conv=t, tt_prep=m)


def kernel(x_prompt, x_sample, cache_diff_k, cache_diff_v, cache_dsa_k, cache_dsa_v, cache_dsa_kidx,
           state_sconv, state_cconv, page_table, norm_mix, w_in, conv_sc, qk_norm, diff_lambda, diff_subln,
           conv_conf, conv_conf_bias, conf_ln, w_gate, w_branch, w_out, norm_ffn, w_ffn_gate, w_ffn_up,
           w_ffn_down):
    bp, tp, d = x_prompt.shape
    bs, ts, _ = x_sample.shape
    depth = w_in.shape[0]
    bw = d // 4
    ffn = w_ffn_gate.shape[-1]
    past = page_table.shape[1] * PAGE_SIZE
    pos_p = jnp.arange(tp, dtype=jnp.int32)
    pos_s = past + jnp.arange(ts, dtype=jnp.int32)
    sc0 = jnp.zeros((bp,) + state_sconv.shape[2:], F32)
    cf0 = jnp.zeros((bp,) + state_cconv.shape[2:], F32)
    tiles_p = _tiles(bp * tp, tp, ffn)
    tiles_s = _tiles(bs * ts, ts, ffn)
    assert bw == GROUPS * HEAD_DIM and cache_diff_k.shape[2] == PAGE_SIZE

    lead = cache_diff_k.shape[:2]
    page_diff_k = cache_diff_k.reshape(lead + (PAGE_ROWS, HEAD_DIM))
    page_diff_v = _value_rows(cache_diff_v, lead)
    page_dsa_k = cache_dsa_k.reshape(lead + (PAGE_ROWS, HEAD_DIM))
    page_dsa_v = cache_dsa_v.reshape(lead + (PAGE_ROWS, HEAD_DIM))
    page_kidx_t = jnp.swapaxes(cache_dsa_kidx, 2, 3)

    w_in_b = _prep_w_in(w_in, bw)
    w_gate_b, w_branch_b, w_out_b, w_ffn_down_b = (a.astype(BF16) for a in (w_gate, w_branch, w_out, w_ffn_down))

    xp = x_prompt.reshape(bp * tp, d)
    xs = x_sample.reshape(bs * ts, d)
    rows_p, rows_s, sc_p, sc_s, cf_p, cf_s = [], [], [], [], [], []
    for l in range(depth):
        lam_init = 0.8 - 0.6 * math.exp(-0.3 * l)
        w = dict(norm_mix=norm_mix[l], w_in=w_in_b, conv_sc=conv_sc[l], qk_norm=qk_norm[l],
                 conv_conf=conv_conf[l], conv_conf_bias=conv_conf_bias[l], conf_ln=conf_ln[l],
                 w_gate=w_gate_b, w_branch=w_branch_b, w_out=w_out_b, norm_ffn=norm_ffn[l],
                 w_ffn_gate=w_ffn_gate, w_ffn_up=w_ffn_up, layer=l, w_ffn_down=w_ffn_down_b)
        dl, sub = diff_lambda[l], diff_subln[l]

        def diff_p(dq, dk, dv):
            return _diff_attn_prompt(dq, dk, dv, dl, sub, bp, tp, lam_init, _pick_tile(tp, (256, 128)))

        def dsa_p(sq, sk, sv, iq, iklo, ikhi, ikw):
            return _dsa_attn_prompt(sq, sk, sv, iq, iklo, ikhi, ikw, bp, tp, _pick_tile(tp, (256, 128)))

        xp, scn, cfn, rows = _layer(xp, bp, tp, pos_p, sc0, cf0, diff_p, dsa_p, w, tiles_p)
        rows_p.append(rows)
        sc_p.append(scn)
        cf_p.append(cfn)

        def diff_s(dq, dk, dv):
            return _diff_attn_sample(dq, dk, dv, page_diff_k, page_diff_v, l, page_table, dl, sub,
                                     bs, ts, lam_init)

        def dsa_s(sq, sk, sv, iq, iklo, ikhi, ikw):
            return _dsa_attn_sample(sq, sk, sv, iq, iklo, ikw, page_dsa_k, page_dsa_v, page_kidx_t,
                                    l, page_table, bs, ts)

        xs, scn, cfn, rows = _layer(xs, bs, ts, pos_s, state_sconv[l], state_cconv[l], diff_s, dsa_s, w,
                                    tiles_s)
        rows_s.append(rows)
        sc_s.append(scn)
        cf_s.append(cfn)

    def stack(rows, i):
        return jnp.stack([r[i] for r in rows])

    return (xp.reshape(bp, tp, d), xs.reshape(bs, ts, d),
            stack(rows_p, 0), stack(rows_s, 0), stack(rows_p, 1), stack(rows_s, 1),
            stack(rows_p, 2), stack(rows_s, 2), stack(rows_p, 3), stack(rows_s, 3),
            stack(rows_p, 4), stack(rows_s, 4),
            jnp.stack(sc_p), jnp.stack(sc_s), jnp.stack(cf_p), jnp.stack(cf_s))
```
